```python
import math
import jax
import jax.numpy as jnp
from jax import lax
import numpy as np

D_MODEL = 1024
BATCH = 32
SEQ = 256
DEPTH = 2
DEC_BATCH = 8
DEC_SEQ = 2048
PAST_LEN = 512

GRID_W = 64
N_EVEN = (DEPTH + 1) // 2
N_ODD = DEPTH // 2
N_MOD = 9
D_FF = 2816
Q_BLOCK = 128
ROPE_THETA = 10000.0
EPS = 1e-6
NEG_INF = -1e30

D_RNN = 512
RNN_BLOCKS = 8
RNN_BW = D_RNN // RNN_BLOCKS
CONV_W = 4
CONV_LEFT = 2
LRU_C = 8.0

DIFF_HEADS = 4
DIFF_HD = 64
DIFF_W = DIFF_HEADS * 2 * DIFF_HD

WIN_HEADS = 16
WIN_KV = 4
WIN_G = WIN_HEADS // WIN_KV
WIN_HD = 64
WINDOW = 128

EVEN_IN = 2 * D_RNN + 3 * DIFF_W
EVEN_MIX = D_RNN + DIFF_W
ODD_IN = (WIN_HEADS + 2 * WIN_KV) * WIN_HD
ODD_MIX = WIN_HEADS * WIN_HD

kernel_name = 'hybrid_diffusion_prefix_trunk_step'


def rmsnorm(x, g):
    xf = x.astype(jnp.float32)
    y = xf * lax.rsqrt(jnp.mean(xf * xf, axis=-1, keepdims=True) + EPS)
    return (y * g.astype(jnp.float32)).astype(x.dtype)


def modulate(x, shift, scale):
    return x * (1 + scale[:, None, :]) + shift[:, None, :]


def swiglu(x, w1, w3, w2):
    return (jax.nn.silu(x @ w1) * (x @ w3)) @ w2


def diff_lambda_init(layer):
    return 0.8 - 0.6 * math.exp(-0.3 * layer)


def grid_angles(n_tok, head_dim):
    rows = n_tok // GRID_W
    row = jnp.repeat(jnp.arange(rows), GRID_W).astype(jnp.float32)
    col = jnp.tile(jnp.arange(GRID_W), rows).astype(jnp.float32)
    half = head_dim // 2
    inv = ROPE_THETA ** (-jnp.arange(0, half, 2, dtype=jnp.float32) / half)
    return row[:, None] * inv[None, :], col[:, None] * inv[None, :]


def rope_rotate(x, ang):
    n = x.shape[-1] // 2
    x1, x2 = x[..., :n], x[..., n:]
    c = jnp.cos(ang).astype(x.dtype)
    s = jnp.sin(ang).astype(x.dtype)
    return jnp.concatenate([x1 * c - x2 * s, x2 * c + x1 * s], axis=-1)


def axial_rope(x, ang_row, ang_col):
    shape = (1, x.shape[1]) + (1,) * (x.ndim - 3) + (ang_row.shape[-1],)
    half = x.shape[-1] // 2
    return jnp.concatenate([rope_rotate(x[..., :half], ang_row.reshape(shape)),
                            rope_rotate(x[..., half:], ang_col.reshape(shape))], axis=-1)


def sweep_queries(fn, q):
    B, S = q.shape[0], q.shape[1]
    nb = S // Q_BLOCK
    qb = jnp.moveaxis(q.reshape((B, nb, Q_BLOCK) + q.shape[2:]), 1, 0)
    out = lax.map(lambda args: fn(args[0], args[1]), (jnp.arange(nb), qb))
    return jnp.moveaxis(out, 0, 1).reshape((B, S) + out.shape[3:])


def centred_dwconv(x, w, b):
    S = x.shape[1]
    xp = jnp.pad(x, ((0, 0), (CONV_LEFT, CONV_W - 1 - CONV_LEFT), (0, 0)))
    y = b
    for tap in range(CONV_W):
        y = y + xp[:, tap:tap + S] * w[tap]
    return y


def _lin_combine(left, right):
    a1, b1 = left
    a2, b2 = right
    return a1 * a2, a2 * b1 + b2


def rglru_scan(xc, wa, ba, wi, bi, lam, h0, reverse):
    B, S, _ = xc.shape
    xb = xc.reshape(B, S, RNN_BLOCKS, RNN_BW)
    r = jax.nn.sigmoid(jnp.einsum('bsnk,nkj->bsnj', xb, wa).reshape(B, S, D_RNN) + ba)
    i = jax.nn.sigmoid(jnp.einsum('bsnk,nkj->bsnj', xb, wi).reshape(B, S, D_RNN) + bi)
    log_a = -LRU_C * r.astype(jnp.float32) * jax.nn.softplus(-lam.astype(jnp.float32))
    a = jnp.exp(log_a)
    u = jnp.sqrt(-jnp.expm1(2.0 * log_a)) * (i * xc).astype(jnp.float32)
    A, Bc = lax.associative_scan(_lin_combine, (a, u), reverse=reverse, axis=1)
    h = A * h0.astype(jnp.float32)[:, None, :] + Bc
    last = h[:, 0] if reverse else h[:, -1]
    return h, last


def diff_attn_block(qb, k, v, lam):
    s = jnp.einsum('bqhmd,bthmd->bhmqt', qb, k).astype(jnp.float32)
    p = jax.nn.softmax(s, axis=-1)
    w = (p[:, :, 0] - lam * p[:, :, 1]).astype(v.dtype)
    return jnp.einsum('bhqt,bthe->bqhe', w, v)


def gqa_sink_block(qb, k, v, mask, sink):
    s = jnp.einsum('bqkgd,btkd->bkgqt', qb, k).astype(jnp.float32)
    if mask is not None:
        s = jnp.where(mask, s, NEG_INF)
    sink_b = sink[None, :, :, None]
    m = jnp.maximum(jnp.max(s, axis=-1), sink_b)
    p = jnp.exp(s - m[..., None])
    denom = jnp.sum(p, axis=-1) + jnp.exp(sink_b - m)
    w = (p / denom[..., None]).astype(v.dtype)
    return jnp.einsum('bkgqt,btkd->bqkgd', w, v)


def even_mixer(h, w_in, w_out, conv_w, conv_b, lru_wa, lru_ba, lru_wi, lru_bi, lru_lam,
               q_g, k_g, lam_vec, subln_g, lam_init, h0, ctx_kv, ang):
    B, S, _ = h.shape
    xr, gt, q, k, v = jnp.split(h @ w_in, [D_RNN, 2 * D_RNN, 2 * D_RNN + DIFF_W, 2 * D_RNN + 2 * DIFF_W], axis=-1)
    xc = centred_dwconv(xr, conv_w, conv_b)
    h_fwd, last_fwd = rglru_scan(xc, lru_wa[0], lru_ba[0], lru_wi[0], lru_bi[0], lru_lam[0], h0[:, 0], False)
    h_bwd, last_bwd = rglru_scan(xc, lru_wa[1], lru_ba[1], lru_wi[1], lru_bi[1], lru_lam[1], h0[:, 1], True)
    y_rnn = (h_fwd + h_bwd).astype(h.dtype) * jax.nn.gelu(gt)
    q = rmsnorm(q.reshape(B, S, DIFF_HEADS, 2, DIFF_HD), q_g)
    k = rmsnorm(k.reshape(B, S, DIFF_HEADS, 2, DIFF_HD), k_g)
    v = v.reshape(B, S, DIFF_HEADS, 2 * DIFF_HD)
    lf = lam_vec.astype(jnp.float32)
    lam = jnp.exp(jnp.sum(lf[0] * lf[1])) - jnp.exp(jnp.sum(lf[2] * lf[3])) + lam_init
    if ctx_kv is None:
        k_all, v_all = k, v
        new_ctx = (k.reshape(B, S, DIFF_HEADS, 2 * DIFF_HD), v,
                   jnp.stack([last_fwd, last_bwd], axis=1).astype(h.dtype))
    else:
        q = axial_rope(q, ang[0], ang[1])
        k = axial_rope(k, ang[0], ang[1])
        ck, cv = ctx_kv
        k_all = jnp.concatenate([ck.reshape(B, ck.shape[1], DIFF_HEADS, 2, DIFF_HD), k], axis=1)
        v_all = jnp.concatenate([cv, v], axis=1)
        new_ctx = None
    q = q * (DIFF_HD ** -0.5)
    o = sweep_queries(lambda bi_, qb: diff_attn_block(qb, k_all, v_all, lam), q)
    o = rmsnorm(o, subln_g) * (1.0 - lam_init)
    out = jnp.concatenate([y_rnn, o.reshape(B, S, DIFF_W)], axis=-1) @ w_out
    return out, new_ctx


def odd_mixer(h, w_in, w_out, q_g, k_g, sink, ctx_kv, ang):
    B, S, _ = h.shape
    q, k, v = jnp.split(h @ w_in, [WIN_HEADS * WIN_HD, (WIN_HEADS + WIN_KV) * WIN_HD], axis=-1)
    q = rmsnorm(q.reshape(B, S, WIN_KV, WIN_G, WIN_HD), q_g)
    k = rmsnorm(k.reshape(B, S, WIN_KV, WIN_HD), k_g)
    v = v.reshape(B, S, WIN_KV, WIN_HD)
    sk = sink.reshape(WIN_KV, WIN_G).astype(jnp.float32)
    scale = WIN_HD ** -0.5
    if ctx_kv is None:
        q = q * scale
        o = sweep_queries(lambda bi_, qb: gqa_sink_block(qb, k, v, None, sk), q)
        new_ctx = (k, v)
    else:
        q = axial_rope(q, ang[0], ang[1]) * scale
        k = axial_rope(k, ang[0], ang[1])
        ck, cv = ctx_kv
        L = ck.shape[1]
        pad = ((0, 0), (Q_BLOCK, Q_BLOCK), (0, 0), (0, 0))
        kp = jnp.pad(k, pad)
        vp = jnp.pad(v, pad)
        span = 3 * Q_BLOCK

        def band_block(bi_, qb):
            start = bi_ * Q_BLOCK
            kl = lax.dynamic_slice_in_dim(kp, start, span, axis=1)
            vl = lax.dynamic_slice_in_dim(vp, start, span, axis=1)
            qpos = start + jnp.arange(Q_BLOCK)
            kpos = start - Q_BLOCK + jnp.arange(span)
            valid = (jnp.abs(qpos[:, None] - kpos[None, :]) <= WINDOW) & (kpos >= 0)[None, :] & (kpos < S)[None, :]
            mask = jnp.concatenate([valid, jnp.ones((Q_BLOCK, L), dtype=bool)], axis=1)
            return gqa_sink_block(qb, jnp.concatenate([kl, ck], axis=1), jnp.concatenate([vl, cv], axis=1), mask, sk)

        o = sweep_queries(band_block, q)
        new_ctx = None
    return o.reshape(B, S, ODD_MIX) @ w_out, new_ctx


def setup_inputs(seed: int = 0) -> dict:
    key = jax.random.key(seed)
    ks = iter(jax.random.split(key, 48))

    def nrm(shape, scale):
        return scale * jax.random.normal(next(ks), shape, jnp.float32)

    def gain(shape):
        return 1.0 + nrm(shape, 0.02)

    u = jax.random.uniform(next(ks), (N_EVEN, 2, D_RNN), jnp.float32, 0.9, 0.999)
    return {
        'x_prompt': nrm((BATCH, SEQ, D_MODEL), 1.0),
        'x_sample': nrm((DEC_BATCH, DEC_SEQ, D_MODEL), 1.0),
        'cache_diff_k': nrm((DEC_BATCH, N_EVEN, PAST_LEN, DIFF_HEADS, 2 * DIFF_HD), 1.0),
        'cache_diff_v': nrm((DEC_BATCH, N_EVEN, PAST_LEN, DIFF_HEADS, 2 * DIFF_HD), 1.0),
        'state_lru': nrm((DEC_BATCH, N_EVEN, 2, D_RNN), 0.5),
        'cache_win_k': nrm((DEC_BATCH, N_ODD, PAST_LEN, WIN_KV, WIN_HD), 1.0),
        'cache_win_v': nrm((DEC_BATCH, N_ODD, PAST_LEN, WIN_KV, WIN_HD), 1.0),
        'c': nrm((DEC_BATCH, D_MODEL), 1.0),
        'c_ctx': nrm((D_MODEL,), 1.0),
        'norm_g': gain((DEPTH, 3, D_MODEL)),
        'w_mod': nrm((DEPTH, D_MODEL, N_MOD * D_MODEL), 0.5 * D_MODEL ** -0.5),
        'b_mod': nrm((DEPTH, N_MOD * D_MODEL), 0.01),
        'ffn_w1': nrm((DEPTH, 2, D_MODEL, D_FF), D_MODEL ** -0.5),
        'ffn_w3': nrm((DEPTH, 2, D_MODEL, D_FF), D_MODEL ** -0.5),
        'ffn_w2': nrm((DEPTH, 2, D_FF, D_MODEL), D_FF ** -0.5),
        'e_w_in': nrm((N_EVEN, D_MODEL, EVEN_IN), D_MODEL ** -0.5),
        'e_w_out': nrm((N_EVEN, EVEN_MIX, D_MODEL), EVEN_MIX ** -0.5),
        'e_conv_w': nrm((N_EVEN, CONV_W, D_RNN), CONV_W ** -0.5),
        'e_conv_b': nrm((N_EVEN, D_RNN), 0.01),
        'e_lru_wa': nrm((N_EVEN, 2, RNN_BLOCKS, RNN_BW, RNN_BW), RNN_BW ** -0.5),
        'e_lru_ba': nrm((N_EVEN, 2, D_RNN), 0.01),
        'e_lru_wi': nrm((N_EVEN, 2, RNN_BLOCKS, RNN_BW, RNN_BW), RNN_BW ** -0.5),
        'e_lru_bi': nrm((N_EVEN, 2, D_RNN), 0.01),
        'e_lru_lam': jnp.log(u) - jnp.log1p(-u),
        'e_q_g': gain((N_EVEN, DIFF_HD)),
        'e_k_g': gain((N_EVEN, DIFF_HD)),
        'e_lam': nrm((N_EVEN, 4, DIFF_HD), 0.1),
        'e_subln_g': gain((N_EVEN, 2 * DIFF_HD)),
        'o_w_in': nrm((N_ODD, D_MODEL, ODD_IN), D_MODEL ** -0.5),
        'o_w_out': nrm((N_ODD, ODD_MIX, D_MODEL), ODD_MIX ** -0.5),
        'o_q_g': gain((N_ODD, WIN_HD)),
        'o_k_g': gain((N_ODD, WIN_HD)),
        'o_sink': nrm((N_ODD, WIN_HEADS), 1.0),
    }


def reference(x_prompt, x_sample, cache_diff_k, cache_diff_v, state_lru, cache_win_k, cache_win_v, c,
              c_ctx, norm_g, w_mod, b_mod, ffn_w1, ffn_w3, ffn_w2,
              e_w_in, e_w_out, e_conv_w, e_conv_b, e_lru_wa, e_lru_ba, e_lru_wi, e_lru_bi, e_lru_lam,
              e_q_g, e_k_g, e_lam, e_subln_g,
              o_w_in, o_w_out, o_q_g, o_k_g, o_sink):

    def trunk(x, cond, latent):
        B, S, _ = x.shape
        ang = grid_angles(S, DIFF_HD) if latent else None
        ctx_out = ([], [], [], [], [])
        for l in range(DEPTH):
            j = l // 2
            mod = jax.nn.silu(cond) @ w_mod[l] + b_mod[l]
            sh1, sc1, g1, sh2, sc2, g2, sh3, sc3, g3 = jnp.split(mod, N_MOD, axis=-1)
            x = x + 0.5 * g1[:, None] * swiglu(modulate(rmsnorm(x, norm_g[l, 0]), sh1, sc1),
                                               ffn_w1[l, 0], ffn_w3[l, 0], ffn_w2[l, 0])
            h = modulate(rmsnorm(x, norm_g[l, 1]), sh2, sc2)
            if l % 2 == 0:
                if latent:
                    h0 = state_lru[:, j]
                    ctx_kv = (cache_diff_k[:, j], cache_diff_v[:, j])
                else:
                    h0 = jnp.zeros((B, 2, D_RNN), jnp.float32)
                    ctx_kv = None
                out, new_ctx = even_mixer(h, e_w_in[j], e_w_out[j], e_conv_w[j], e_conv_b[j],
                                          e_lru_wa[j], e_lru_ba[j], e_lru_wi[j], e_lru_bi[j], e_lru_lam[j],
                                          e_q_g[j], e_k_g[j], e_lam[j], e_subln_g[j], diff_lambda_init(l),
                                          h0, ctx_kv, ang)
                if not latent:
                    ctx_out[0].append(new_ctx[0])
                    ctx_out[1].append(new_ctx[1])
                    ctx_out[2].append(new_ctx[2])
            else:
                ctx_kv = (cache_win_k[:, j], cache_win_v[:, j]) if latent else None
                out, new_ctx = odd_mixer(h, o_w_in[j], o_w_out[j], o_q_g[j], o_k_g[j], o_sink[j], ctx_kv, ang)
                if not latent:
                    ctx_out[3].append(new_ctx[0])
                    ctx_out[4].append(new_ctx[1])
            x = x + g2[:, None] * out
            x = x + 0.5 * g3[:, None] * swiglu(modulate(rmsnorm(x, norm_g[l, 2]), sh3, sc3),
                                               ffn_w1[l, 1], ffn_w3[l, 1], ffn_w2[l, 1])
        return x, ctx_out

    y_prompt, ctx_out = trunk(x_prompt, c_ctx[None, :], False)
    y_sample, _ = trunk(x_sample, c, True)
    new_diff_k = jnp.stack(ctx_out[0], axis=1)
    new_diff_v = jnp.stack(ctx_out[1], axis=1)
    new_state_lru = jnp.stack(ctx_out[2], axis=1)
    new_win_k = jnp.stack(ctx_out[3], axis=1)
    new_win_v = jnp.stack(ctx_out[4], axis=1)
    return (y_prompt, y_sample, new_diff_k, new_diff_v, new_state_lru, new_win_k, new_win_v)
```

```python
import functools
import math

import numpy as np
import jax
import jax.numpy as jnp
from jax import lax
from jax.experimental import pallas as pl
from jax.experimental.pallas import tpu as pltpu

F32 = jnp.float32
BF16 = jnp.bfloat16

D_MODEL = 1024
DEPTH = 2
N_MOD = 9
D_FF = 2816
GRID_W = 64
ROPE_THETA = 10000.0
EPS = 1e-6
NEG_INF = -1e30

D_RNN = 512
RNN_BLOCKS = 8
RNN_BW = D_RNN // RNN_BLOCKS
LRU_C = 8.0

DIFF_HEADS = 4
DIFF_HD = 64
DIFF_W = DIFF_HEADS * 2 * DIFF_HD

WIN_HEADS = 16
WIN_KV = 4
WIN_G = WIN_HEADS // WIN_KV
WIN_HD = 64
WINDOW = 128

EVEN_IN = 2 * D_RNN + 3 * DIFF_W
ODD_IN = (WIN_HEADS + 2 * WIN_KV) * WIN_HD
ODD_MIX = WIN_HEADS * WIN_HD

LANES = 128
SUBLANES = 8
HEAD_HALF = LANES // 2
ROPE_PAIR = DIFF_HD // 4

TOKEN_TILE = 512
FF_CHUNK = 256
MOD_COLS = 1152
COND_ROWS = 16
LRU_CHUNK = 256
DIFF_Q_TILE = 256
WIN_Q_TILE = 128
VMEM_LIMIT = 56 * 1024 * 1024


def _cparams(*sem):
    return pltpu.CompilerParams(dimension_semantics=sem, vmem_limit_bytes=VMEM_LIMIT)


def _resident(shape):
    return pl.BlockSpec(shape, lambda *_: (0,) * len(shape), pipeline_mode=pl.Buffered(1))


def _dot(a, b):
    return jnp.dot(a, b, preferred_element_type=F32)


def _dot_nt(a, b):
    return lax.dot_general(a, b, (((1,), (1,)), ((), ())), preferred_element_type=F32)


def _rms_mod(x, ng, shift, scale):
    ms = jnp.mean(x * x, axis=-1, keepdims=True)
    y = x * lax.rsqrt(ms + EPS) * ng
    return y * (1.0 + scale) + shift


def _silu(a):
    return a * jax.nn.sigmoid(a)


def _lo_mask():
    return lax.broadcasted_iota(jnp.int32, (1, LANES), 1) < HEAD_HALF


def _head_norm(t, g):
    lo = _lo_mask()
    sq = t * t
    s_lo = jnp.sum(jnp.where(lo, sq, 0.0), axis=-1, keepdims=True)
    s_hi = jnp.sum(jnp.where(lo, 0.0, sq), axis=-1, keepdims=True)
    inv = jnp.where(lo, lax.rsqrt(s_lo / DIFF_HD + EPS), lax.rsqrt(s_hi / DIFF_HD + EPS))
    return t * inv * g


def _rope(t, cos, sin_signed):
    lane = lax.broadcasted_iota(jnp.int32, (1, LANES), 1)
    first = (lane % (2 * ROPE_PAIR)) < ROPE_PAIR
    partner = jnp.where(first, pltpu.roll(t, LANES - ROPE_PAIR, axis=1), pltpu.roll(t, ROPE_PAIR, axis=1))
    return t * cos + partner * sin_signed


def _swap_halves(t):
    return pltpu.roll(t, HEAD_HALF, axis=1)


def _dup_half(t, which):
    lo = _lo_mask()
    r = _swap_halves(t)
    return jnp.where(lo, t, r) if which == 0 else jnp.where(lo, r, t)


def _mod_kernel(c_ref, w_ref, b_ref, o_ref):
    c = c_ref[...]
    s = _silu(c).astype(BF16)
    o_ref[...] = _dot(s, w_ref[...].astype(BF16)) + b_ref[...]


def _modulation(cond, w_mod, b_mod):
    n_col = N_MOD * D_MODEL
    return pl.pallas_call(
        _mod_kernel,
        grid=(DEPTH, n_col // MOD_COLS),
        in_specs=[
            pl.BlockSpec((COND_ROWS, D_MODEL), lambda l, j: (0, 0)),
            pl.BlockSpec((None, D_MODEL, MOD_COLS), lambda l, j: (l, 0, j)),
            pl.BlockSpec((None, 1, MOD_COLS), lambda l, j: (l, 0, j)),
        ],
        out_specs=pl.BlockSpec((None, COND_ROWS, MOD_COLS), lambda l, j: (l, 0, j)),
        out_shape=jax.ShapeDtypeStruct((DEPTH, COND_ROWS, n_col), F32),
        compiler_params=_cparams("arbitrary", "arbitrary"),
        name="modulation",
    )(cond, w_mod, b_mod.reshape(DEPTH, 1, n_col))


def _tok_spec(width, tm):
    return pl.BlockSpec((tm, width), lambda i: (i, 0))


def _mod_spec(tm, tokens_per_row):
    return pl.BlockSpec((None, N_MOD, D_MODEL), lambda i: ((i * tm) // tokens_per_row, 0, 0))


def _ffn_kernel(x_ref, m_ref, ng_ref, w1_ref, w3_ref, w2_ref, o_ref, g_scr, *, mi):
    x = x_ref[...]
    h = _rms_mod(x, ng_ref[...], m_ref[mi:mi + 1, :], m_ref[mi + 1:mi + 2, :]).astype(BF16)
    for j in range(D_FF // FF_CHUNK):
        cols = slice(j * FF_CHUNK, (j + 1) * FF_CHUNK)
        a = _dot(h, w1_ref[:, cols])
        b = _dot(h, w3_ref[:, cols])
        g_scr[:, cols] = (_silu(a) * b).astype(BF16)
    y = _dot(g_scr[...], w2_ref[...])
    o_ref[...] = x + 0.5 * m_ref[mi + 2:mi + 3, :] * y


def _ffn(x2d, mrows, tokens_per_row, ng, w1, w3, w2, mi):
    n = x2d.shape[0]
    tm = TOKEN_TILE
    return pl.pallas_call(
        functools.partial(_ffn_kernel, mi=mi),
        grid=(n // tm,),
        in_specs=[
            _tok_spec(D_MODEL, tm),
            _mod_spec(tm, tokens_per_row),
            _resident((1, D_MODEL)),
            _resident((D_MODEL, D_FF)),
            _resident((D_MODEL, D_FF)),
            _resident((D_FF, D_MODEL)),
        ],
        out_specs=_tok_spec(D_MODEL, tm),
        out_shape=jax.ShapeDtypeStruct((n, D_MODEL), F32),
        scratch_shapes=[pltpu.VMEM((tm, D_FF), BF16)],
        compiler_params=_cparams("arbitrary"),
        name="swiglu",
    )(x2d, mrows, ng, w1, w3, w2)


def _out_kernel(*refs, n_act):
    x_ref, m_ref = refs[0], refs[1]
    acts = refs[2:2 + n_act]
    w_ref, o_ref = refs[2 + n_act], refs[3 + n_act]
    y = None
    row = 0
    for a_ref in acts:
        ka = a_ref.shape[1]
        part = _dot(a_ref[...], w_ref[row:row + ka, :])
        y = part if y is None else y + part
        row += ka
    o_ref[...] = x_ref[...] + m_ref[5:6, :] * y


def _mixer_out(x2d, mrows, tokens_per_row, acts, w_out):
    n = x2d.shape[0]
    tm = TOKEN_TILE
    return pl.pallas_call(
        functools.partial(_out_kernel, n_act=len(acts)),
        grid=(n // tm,),
        in_specs=[_tok_spec(D_MODEL, tm), _mod_spec(tm, tokens_per_row)]
        + [_tok_spec(a.shape[1], tm) for a in acts]
        + [_resident(w_out.shape)],
        out_specs=_tok_spec(D_MODEL, tm),
        out_shape=jax.ShapeDtypeStruct((n, D_MODEL), F32),
        compiler_params=_cparams("arbitrary"),
        name="mixer_out",
    )(x2d, mrows, *acts, w_out)


def _even_in_kernel(*refs, latent):
    if latent:
        x_ref, m_ref, ng_ref, w_ref, qg_ref, kg_ref, cos_ref, sin_ref = refs[:8]
        outs = refs[8:]
    else:
        x_ref, m_ref, ng_ref, w_ref, qg_ref, kg_ref = refs[:6]
        outs = refs[6:]
    xr_ref, gt_ref, q_ref, k_ref, v_ref = outs
    h = _rms_mod(x_ref[...], ng_ref[...], m_ref[3:4, :], m_ref[4:5, :]).astype(BF16)
    xr_ref[...] = _dot(h, w_ref[:, 0:D_RNN])
    gt_ref[...] = _dot(h, w_ref[:, D_RNN:2 * D_RNN])
    base = 2 * D_RNN
    q = _dot(h, w_ref[:, base:base + DIFF_W])
    k = _dot(h, w_ref[:, base + DIFF_W:base + 2 * DIFF_W])
    v_ref[...] = _dot(h, w_ref[:, base + 2 * DIFF_W:base + 3 * DIFF_W]).astype(v_ref.dtype)
    for hh in range(DIFF_HEADS):
        cols = slice(hh * LANES, (hh + 1) * LANES)
        qh = _head_norm(q[:, cols], qg_ref[...])
        kh = _head_norm(k[:, cols], kg_ref[...])
        if latent:
            qh = _rope(qh, cos_ref[...], sin_ref[...])
            kh = _rope(kh, cos_ref[...], sin_ref[...])
        q_ref[:, cols] = (qh * (DIFF_HD ** -0.5)).astype(BF16)
        k_ref[:, cols] = kh.astype(k_ref.dtype)


def _even_in(x2d, mrows, tokens_per_row, ng, w_in, qg, kg, rope):
    n = x2d.shape[0]
    tm = TOKEN_TILE
    latent = rope is not None
    in_specs = [
        _tok_spec(D_MODEL, tm), _mod_spec(tm, tokens_per_row), _resident((1, D_MODEL)),
        _resident((D_MODEL, EVEN_IN)), _resident((1, LANES)), _resident((1, LANES)),
    ]
    args = [x2d, mrows, ng, w_in, qg, kg]
    if latent:
        seq = rope[0].shape[0]
        tab = pl.BlockSpec((tm, LANES), lambda i: (i % (seq // tm), 0))
        in_specs += [tab, tab]
        args += list(rope)
    kv_dtype = BF16 if latent else F32
    return pl.pallas_call(
        functools.partial(_even_in_kernel, latent=latent),
        grid=(n // tm,),
        in_specs=in_specs,
        out_specs=[_tok_spec(D_RNN, tm)] * 2 + [_tok_spec(DIFF_W, tm)] * 3,
        out_shape=[
            jax.ShapeDtypeStruct((n, D_RNN), F32), jax.ShapeDtypeStruct((n, D_RNN), F32),
            jax.ShapeDtypeStruct((n, DIFF_W), BF16),
            jax.ShapeDtypeStruct((n, DIFF_W), kv_dtype), jax.ShapeDtypeStruct((n, DIFF_W), kv_dtype),
        ],
        compiler_params=_cparams("arbitrary"),
        name="even_in",
    )(*args)


def _gelu_tanh(x):
    return 0.5 * x * (1.0 + jnp.tanh(math.sqrt(2.0 / math.pi) * (x + 0.044715 * (x * x * x))))


def _group_scan(a, u, reverse):
    t, c = a.shape
    a3 = a.reshape(t // SUBLANES, SUBLANES, c)
    b3 = u.reshape(t // SUBLANES, SUBLANES, c)
    row = lax.broadcasted_iota(jnp.int32, (1, SUBLANES, c), 1)
    d = 1
    while d < SUBLANES:
        if reverse:
            keep = row < SUBLANES - d
            shift = SUBLANES - d
        else:
            keep = row >= d
            shift = d
        a_prev = jnp.where(keep, pltpu.roll(a3, shift, axis=1), 1.0)
        b_prev = jnp.where(keep, pltpu.roll(b3, shift, axis=1), 0.0)
        b3 = a3 * b_prev + b3
        a3 = a3 * a_prev
        d *= 2
    return a3.reshape(t, c), b3.reshape(t, c)


def _lru_kernel(xr_ref, gt_ref, cw_ref, cb_ref, wg_ref, bg_ref, lam_ref, h0_ref,
                y_ref, last_ref, xc_scr, hf_scr, a_scr, b_scr, *, seq, tc):
    n_chunks = seq // tc
    n_groups = tc // SUBLANES
    lam = lam_ref[...]
    sp = jnp.maximum(-lam, 0.0) + jnp.log1p(jnp.exp(-jnp.abs(lam)))

    def conv_chunk(c):
        r0 = pl.multiple_of(c * tc, tc)
        p0 = pl.multiple_of(jnp.maximum(r0 - SUBLANES, 0), SUBLANES)
        n0 = pl.multiple_of(jnp.minimum(r0 + tc, seq - SUBLANES), SUBLANES)
        prev = jnp.where(c > 0, xr_ref[pl.ds(p0, SUBLANES), :], 0.0)
        nxt = jnp.where(c < n_chunks - 1, xr_ref[pl.ds(n0, SUBLANES), :], 0.0)
        xw = jnp.concatenate([prev, xr_ref[pl.ds(r0, tc), :], nxt], axis=0)
        xc = cb_ref[...]
        for tap in range(4):
            off = SUBLANES - 2 + tap
            xc = xc + xw[off:off + tc, :] * cw_ref[tap:tap + 1, :]
        return xc

    def gates(xc, d):
        cols = slice(d * 2 * D_RNN, (d + 1) * 2 * D_RNN)
        z = _dot(xc.astype(BF16), wg_ref[:, cols]) + bg_ref[:, cols]
        r = jax.nn.sigmoid(z[:, :D_RNN])
        i = jax.nn.sigmoid(z[:, D_RNN:])
        log_a = -LRU_C * r * sp[d:d + 1, :]
        a = jnp.exp(log_a)
        th = jnp.tanh(log_a)
        u = jnp.sqrt(-2.0 * th / (1.0 - th)) * (i * xc)
        return a, u

    def carry_loop(carry, dst_ref, dst_base, reverse):
        def body(g, carry):
            gi = (n_groups - 1 - g) if reverse else g
            r0 = pl.multiple_of(gi * SUBLANES, SUBLANES)
            h8 = a_scr[pl.ds(r0, SUBLANES), :] * carry + b_scr[pl.ds(r0, SUBLANES), :]
            dst_ref[pl.ds(pl.multiple_of(dst_base + r0, SUBLANES), SUBLANES), :] = h8
            return h8[0:1, :] if reverse else h8[SUBLANES - 1:SUBLANES, :]
        return lax.fori_loop(0, n_groups, body, carry, unroll=4)

    def fwd_chunk(c, carry):
        r0 = pl.multiple_of(c * tc, tc)
        xc = conv_chunk(c)
        xc_scr[pl.ds(r0, tc), :] = xc
        a, u = gates(xc, 0)
        a_cum, b_cum = _group_scan(a, u, False)
        a_scr[...] = a_cum
        b_scr[...] = b_cum
        return carry_loop(carry, hf_scr, r0, False)

    def bwd_chunk(cc, carry):
        c = n_chunks - 1 - cc
        r0 = pl.multiple_of(c * tc, tc)
        a, u = gates(xc_scr[pl.ds(r0, tc), :], 1)
        a_cum, b_cum = _group_scan(a, u, True)
        a_scr[...] = a_cum
        b_scr[...] = b_cum
        carry = carry_loop(carry, b_scr, 0, True)
        hsum = hf_scr[pl.ds(r0, tc), :] + b_scr[...]
        y_ref[pl.ds(r0, tc), :] = (hsum * _gelu_tanh(gt_ref[pl.ds(r0, tc), :])).astype(y_ref.dtype)
        return carry

    last_f = lax.fori_loop(0, n_chunks, fwd_chunk, h0_ref[0:1, :])
    last_b = lax.fori_loop(0, n_chunks, bwd_chunk, h0_ref[1:2, :])
    last_ref[0:1, :] = last_f
    last_ref[1:2, :] = last_b


def _lru(xr, gt, conv_w, conv_b, wg, bg, lam, h0):
    b, seq, _ = xr.shape
    tc = LRU_CHUNK
    seq_spec = pl.BlockSpec((None, seq, D_RNN), lambda i: (i, 0, 0))
    st_spec = pl.BlockSpec((None, 2, D_RNN), lambda i: (i, 0, 0))
    return pl.pallas_call(
        functools.partial(_lru_kernel, seq=seq, tc=tc),
        grid=(b,),
        in_specs=[seq_spec, seq_spec, _resident((4, D_RNN)), _resident((1, D_RNN)),
                  _resident((D_RNN, 4 * D_RNN)), _resident((1, 4 * D_RNN)), _resident((2, D_RNN)), st_spec],
        out_specs=[seq_spec, st_spec],
        out_shape=[jax.ShapeDtypeStruct((b, seq, D_RNN), BF16), jax.ShapeDtypeStruct((b, 2, D_RNN), F32)],
        scratch_shapes=[pltpu.VMEM((seq, D_RNN), F32), pltpu.VMEM((seq, D_RNN), F32),
                        pltpu.VMEM((tc, D_RNN), F32), pltpu.VMEM((tc, D_RNN), F32)],
        compiler_params=_cparams("arbitrary"),
        name="rglru",
    )(xr, gt, conv_w, conv_b, wg, bg, lam, h0)


def _diff_attn_kernel(*refs, n_src, lam_init):
    q_ref, lamv_ref, sg_ref = refs[:3]
    srcs = [(refs[3 + 2 * s], refs[4 + 2 * s]) for s in range(n_src)]
    o_ref = refs[3 + 2 * n_src]
    lv = lamv_ref[...]
    lam = (jnp.exp(jnp.sum(lv[0:1, :] * lv[1:2, :], axis=-1, keepdims=True))
           - jnp.exp(jnp.sum(lv[2:3, :] * lv[3:4, :], axis=-1, keepdims=True)) + lam_init)
    lo = _lo_mask()
    zero = jnp.zeros((), BF16)
    for hh in range(DIFF_HEADS):
        cols = slice(hh * LANES, (hh + 1) * LANES)
        qh = q_ref[:, cols]
        q1 = jnp.where(lo, qh, zero)
        q2 = jnp.where(lo, zero, qh)
        ks = [k_ref[:, cols].astype(BF16) for k_ref, _ in srcs]
        s1 = [_dot_nt(q1, kh) for kh in ks]
        s2 = [_dot_nt(q2, kh) for kh in ks]

        def probs(ss):
            m = functools.reduce(jnp.maximum, [jnp.max(s, axis=-1, keepdims=True) for s in ss])
            ps = [jnp.exp(s - m) for s in ss]
            den = functools.reduce(jnp.add, [jnp.sum(p, axis=-1, keepdims=True) for p in ps])
            return ps, 1.0 / den

        p1, r1 = probs(s1)
        p2, r2 = probs(s2)
        r2 = lam * r2
        o = None
        for idx, (_, v_ref) in enumerate(srcs):
            w = (p1[idx] * r1 - p2[idx] * r2).astype(BF16)
            part = _dot(w, v_ref[:, cols].astype(BF16))
            o = part if o is None else o + part
        ms = jnp.mean(o * o, axis=-1, keepdims=True)
        o = o * lax.rsqrt(ms + EPS) * sg_ref[...] * (1.0 - lam_init)
        o_ref[:, cols] = o.astype(o_ref.dtype)


def _diff_attn(q, lam_vec, subln_g, srcs, lam_init):
    b, seq, _ = q.shape
    tq = min(DIFF_Q_TILE, seq)
    q_spec = pl.BlockSpec((None, tq, DIFF_W), lambda i, j: (i, j, 0))
    in_specs = [q_spec, _resident((4, DIFF_HD)), _resident((1, LANES))]
    args = [q, lam_vec, subln_g]
    for k, v in srcs:
        kv_spec = pl.BlockSpec((None, k.shape[1], DIFF_W), lambda i, j: (i, 0, 0))
        in_specs += [kv_spec, kv_spec]
        args += [k, v]
    return pl.pallas_call(
        functools.partial(_diff_attn_kernel, n_src=len(srcs), lam_init=lam_init),
        grid=(b, seq // tq),
        in_specs=in_specs,
        out_specs=q_spec,
        out_shape=jax.ShapeDtypeStruct((b, seq, DIFF_W), BF16),
        compiler_params=_cparams("arbitrary", "arbitrary"),
        name="diff_attn",
    )(*args)


def _odd_in_kernel(*refs, latent):
    if latent:
        x_ref, m_ref, ng_ref, w_ref, qg_ref, kg_ref, cos_ref, sin_ref = refs[:8]
        q_ref, kd_ref, vd_ref = refs[8:]
    else:
        x_ref, m_ref, ng_ref, w_ref, qg_ref, kg_ref = refs[:6]
        q_ref, kd_ref, vd_ref, k_ref, v_ref = refs[6:]
    h = _rms_mod(x_ref[...], ng_ref[...], m_ref[3:4, :], m_ref[4:5, :]).astype(BF16)
    n_q = WIN_HEADS * WIN_HD
    n_kv = WIN_KV * WIN_HD
    q = _dot(h, w_ref[:, 0:n_q])
    for blk in range(n_q // LANES):
        cols = slice(blk * LANES, (blk + 1) * LANES)
        qh = _head_norm(q[:, cols], qg_ref[...])
        if latent:
            qh = _rope(qh, cos_ref[...], sin_ref[...])
        q_ref[:, cols] = (qh * (WIN_HD ** -0.5)).astype(BF16)
    k = _dot(h, w_ref[:, n_q:n_q + n_kv])
    v = _dot(h, w_ref[:, n_q + n_kv:n_q + 2 * n_kv])
    for blk in range(n_kv // LANES):
        cols = slice(blk * LANES, (blk + 1) * LANES)
        kh = _head_norm(k[:, cols], kg_ref[...])
        vh = v[:, cols]
        if latent:
            kh = _rope(kh, cos_ref[...], sin_ref[...])
        else:
            k_ref[:, cols] = kh
            v_ref[:, cols] = vh
        for half in range(2):
            dst = slice((2 * blk + half) * LANES, (2 * blk + half + 1) * LANES)
            kd_ref[:, dst] = _dup_half(kh, half).astype(BF16)
            vd_ref[:, dst] = _dup_half(vh, half).astype(BF16)


def _odd_in(x2d, mrows, tokens_per_row, ng, w_in, qg, kg, rope):
    n = x2d.shape[0]
    tm = TOKEN_TILE
    latent = rope is not None
    n_q = WIN_HEADS * WIN_HD
    n_kv = WIN_KV * WIN_HD
    in_specs = [
        _tok_spec(D_MODEL, tm), _mod_spec(tm, tokens_per_row), _resident((1, D_MODEL)),
        _resident((D_MODEL, ODD_IN)), _resident((1, LANES)), _resident((1, LANES)),
    ]
    args = [x2d, mrows, ng, w_in, qg, kg]
    out_specs = [_tok_spec(n_q, tm), _tok_spec(2 * n_kv, tm), _tok_spec(2 * n_kv, tm)]
    out_shape = [jax.ShapeDtypeStruct((n, n_q), BF16), jax.ShapeDtypeStruct((n, 2 * n_kv), BF16),
                 jax.ShapeDtypeStruct((n, 2 * n_kv), BF16)]
    if latent:
        seq = rope[0].shape[0]
        tab = pl.BlockSpec((tm, LANES), lambda i: (i % (seq // tm), 0))
        in_specs += [tab, tab]
        args += list(rope)
    else:
        out_specs += [_tok_spec(n_kv, tm)] * 2
        out_shape += [jax.ShapeDtypeStruct((n, n_kv), F32)] * 2
    return pl.pallas_call(
        functools.partial(_odd_in_kernel, latent=latent),
        grid=(n // tm,),
        in_specs=in_specs,
        out_specs=out_specs,
        out_shape=out_shape,
        compiler_params=_cparams("arbitrary"),
        name="odd_in",
    )(*args)


def _win_attn_kernel(*refs, latent, seq, tq):
    if latent:
        q_ref, sink_ref, kd_ref, vd_ref, ck_ref, cv_ref, o_ref = refs
    else:
        q_ref, sink_ref, kd_ref, vd_ref, o_ref = refs
    lo = _lo_mask()
    zero = jnp.zeros((), BF16)
    if latent:
        span = 3 * tq
        i = pl.program_id(1)
        start = pl.multiple_of(jnp.clip((i - 1) * tq, 0, seq - span), tq)
        qpos = i * tq + lax.broadcasted_iota(jnp.int32, (tq, 1), 0)
        kpos = start + lax.broadcasted_iota(jnp.int32, (1, span), 1)
        valid = jnp.abs(qpos - kpos) <= WINDOW
    for j in range(WIN_KV):
        kcols = slice(j * LANES, (j + 1) * LANES)
        if latent:
            cblk = slice((j // 2) * LANES, (j // 2 + 1) * LANES)
            ks = [kd_ref[pl.ds(start, span), kcols],
                  _dup_half(ck_ref[:, cblk], j % 2).astype(BF16)]
            vv = [vd_ref[pl.ds(start, span), kcols],
                  _dup_half(cv_ref[:, cblk], j % 2).astype(BF16)]
        else:
            ks = [kd_ref[:, kcols]]
            vv = [vd_ref[:, kcols]]
        v_lo = [jnp.where(lo, v, zero) for v in vv]
        v_hi = [jnp.where(lo, zero, v) for v in vv]
        for pair in range(WIN_G // 2):
            blk = 2 * j + pair
            qcols = slice(blk * LANES, (blk + 1) * LANES)
            qb = q_ref[:, qcols]
            o = None
            for half in range(2):
                head = 2 * blk + half
                qm = jnp.where(lo, qb, zero) if half == 0 else jnp.where(lo, zero, qb)
                ss = [_dot_nt(qm, kh) for kh in ks]
                if latent:
                    ss[0] = jnp.where(valid, ss[0], NEG_INF)
                sink = sink_ref[head]
                m = functools.reduce(jnp.maximum, [jnp.max(s, axis=-1, keepdims=True) for s in ss])
                m = jnp.maximum(m, sink)
                ps = [jnp.exp(s - m) for s in ss]
                den = functools.reduce(jnp.add, [jnp.sum(p, axis=-1, keepdims=True) for p in ps])
                inv = 1.0 / (den + jnp.exp(sink - m))
                vsel = v_lo if half == 0 else v_hi
                for p, vh in zip(ps, vsel):
                    part = _dot((p * inv).astype(BF16), vh)
                    o = part if o is None else o + part
            o_ref[:, qcols] = o.astype(o_ref.dtype)


def _win_attn(q, sink, kd, vd, ctx):
    b, seq, n_q = q.shape
    latent = ctx is not None
    tq = WIN_Q_TILE
    q_spec = pl.BlockSpec((None, tq, n_q), lambda i, j: (i, j, 0))
    kv_spec = pl.BlockSpec((None, seq, kd.shape[2]), lambda i, j: (i, 0, 0))
    in_specs = [q_spec, pl.BlockSpec(memory_space=pltpu.SMEM), kv_spec, kv_spec]
    args = [q, sink, kd, vd]
    if latent:
        ck, cv = ctx
        c_spec = pl.BlockSpec((None, ck.shape[1], ck.shape[2]), lambda i, j: (i, 0, 0))
        in_specs += [c_spec, c_spec]
        args += [ck, cv]
    return pl.pallas_call(
        functools.partial(_win_attn_kernel, latent=latent, seq=seq, tq=tq),
        grid=(b, seq // tq),
        in_specs=in_specs,
        out_specs=q_spec,
        out_shape=jax.ShapeDtypeStruct((b, seq, n_q), BF16),
        compiler_params=_cparams("arbitrary", "arbitrary"),
        name="win_attn",
    )(*args)


def _rope_tables(seq):
    rows = seq // GRID_W
    row = np.repeat(np.arange(rows), GRID_W).astype(np.float32)
    col = np.tile(np.arange(GRID_W), rows).astype(np.float32)
    half = DIFF_HD // 2
    inv = (ROPE_THETA ** (-np.arange(0, half, 2, dtype=np.float32) / half)).astype(np.float32)
    ang_r = row[:, None] * inv[None, :]
    ang_c = col[:, None] * inv[None, :]
    cos = np.concatenate([np.cos(ang_r), np.cos(ang_r), np.cos(ang_c), np.cos(ang_c)], axis=1)
    sin = np.concatenate([-np.sin(ang_r), np.sin(ang_r), -np.sin(ang_c), np.sin(ang_c)], axis=1)
    cos = np.tile(cos, (1, LANES // DIFF_HD)).astype(np.float32)
    sin = np.tile(sin, (1, LANES // DIFF_HD)).astype(np.float32)
    return jnp.asarray(cos), jnp.asarray(sin)


def _block_diag(w):
    eye = jnp.eye(RNN_BLOCKS, dtype=w.dtype)
    return jnp.einsum('nkj,nm->nkmj', w, eye).reshape(D_RNN, D_RNN)


def _tile_gain(g):
    return jnp.tile(g, LANES // g.shape[0]).reshape(1, LANES)


def _diff_lambda_init(layer):
    return 0.8 - 0.6 * math.exp(-0.3 * layer)


def kernel(x_prompt, x_sample, cache_diff_k, cache_diff_v, state_lru, cache_win_k, cache_win_v, c, c_ctx,
           norm_g, w_mod, b_mod, ffn_w1, ffn_w3, ffn_w2, e_w_in, e_w_out, e_conv_w, e_conv_b,
           e_lru_wa, e_lru_ba, e_lru_wi, e_lru_bi, e_lru_lam, e_q_g, e_k_g, e_lam, e_subln_g,
           o_w_in, o_w_out, o_q_g, o_k_g, o_sink):
    batch, seq, _ = x_prompt.shape
    dec_batch, dec_seq, _ = x_sample.shape
    past = cache_diff_k.shape[2]

    cond = jnp.concatenate([c_ctx[None, :], c], axis=0)
    cond = jnp.pad(cond, ((0, COND_ROWS - cond.shape[0]), (0, 0)))
    mod = _modulation(cond, w_mod, b_mod).reshape(DEPTH, COND_ROWS, N_MOD, D_MODEL)

    w1 = ffn_w1.astype(BF16)
    w3 = ffn_w3.astype(BF16)
    w2 = ffn_w2.astype(BF16)
    rope = _rope_tables(dec_seq)

    groups = [
        dict(x=x_prompt.reshape(batch * seq, D_MODEL), b=batch, s=seq, latent=False,
             rows=slice(0, 1), per_row=batch * seq),
        dict(x=x_sample.reshape(dec_batch * dec_seq, D_MODEL), b=dec_batch, s=dec_seq, latent=True,
             rows=slice(1, 1 + dec_batch), per_row=dec_seq),
    ]
    ctx_out = {}
    finals = []
    for grp in groups:
        x = grp['x']
        nb, s, latent, per_row = grp['b'], grp['s'], grp['latent'], grp['per_row']
        for l in range(DEPTH):
            j = l // 2
            mrows = mod[l, grp['rows']]
            ng = norm_g[l].reshape(3, 1, D_MODEL)
            x = _ffn(x, mrows, per_row, ng[0], w1[l, 0], w3[l, 0], w2[l, 0], 0)
            if l % 2 == 0:
                xr, gt, q, k, v = _even_in(x, mrows, per_row, ng[1], e_w_in[j].astype(BF16),
                                           _tile_gain(e_q_g[j]), _tile_gain(e_k_g[j]), rope if latent else None)
                wg = jnp.concatenate([_block_diag(e_lru_wa[j, 0]), _block_diag(e_lru_wi[j, 0]),
                                      _block_diag(e_lru_wa[j, 1]), _block_diag(e_lru_wi[j, 1])], axis=1).astype(BF16)
                bg = jnp.concatenate([e_lru_ba[j, 0], e_lru_bi[j, 0], e_lru_ba[j, 1], e_lru_bi[j, 1]]).reshape(1, -1)
                h0 = state_lru[:, j] if latent else jnp.zeros((nb, 2, D_RNN), F32)
                y_rnn, last = _lru(xr.reshape(nb, s, D_RNN), gt.reshape(nb, s, D_RNN), e_conv_w[j],
                                   e_conv_b[j].reshape(1, D_RNN), wg, bg, e_lru_lam[j], h0)
                k3 = k.reshape(nb, s, DIFF_W)
                v3 = v.reshape(nb, s, DIFF_W)
                srcs = [(k3, v3)]
                if latent:
                    srcs = [(cache_diff_k[:, j].reshape(nb, past, DIFF_W),
                             cache_diff_v[:, j].reshape(nb, past, DIFF_W))] + srcs
                else:
                    ctx_out.setdefault('diff_k', []).append(k3.reshape(nb, s, DIFF_HEADS, 2 * DIFF_HD))
                    ctx_out.setdefault('diff_v', []).append(v3.reshape(nb, s, DIFF_HEADS, 2 * DIFF_HD))
                    ctx_out.setdefault('state', []).append(last)
                o = _diff_attn(q.reshape(nb, s, DIFF_W), e_lam[j], e_subln_g[j].reshape(1, LANES), srcs,
                               _diff_lambda_init(l))
                acts = [y_rnn.reshape(nb * s, D_RNN), o.reshape(nb * s, DIFF_W)]
                x = _mixer_out(x, mrows, per_row, acts, e_w_out[j].astype(BF16))
            else:
                outs = _odd_in(x, mrows, per_row, ng[1], o_w_in[j].astype(BF16),
                               _tile_gain(o_q_g[j]), _tile_gain(o_k_g[j]), rope if latent else None)
                q, kd, vd = outs[:3]
                n_kv = WIN_KV * WIN_HD
                ctx = None
                if latent:
                    ctx = (cache_win_k[:, j].reshape(nb, past, n_kv), cache_win_v[:, j].reshape(nb, past, n_kv))
                else:
                    ctx_out.setdefault('win_k', []).append(outs[3].reshape(nb, s, WIN_KV, WIN_HD))
                    ctx_out.setdefault('win_v', []).append(outs[4].reshape(nb, s, WIN_KV, WIN_HD))
                o = _win_attn(q.reshape(nb, s, ODD_MIX), o_sink[j], kd.reshape(nb, s, 2 * n_kv),
                              vd.reshape(nb, s, 2 * n_kv), ctx)
                x = _mixer_out(x, mrows, per_row, [o.reshape(nb * s, ODD_MIX)], o_w_out[j].astype(BF16))
            x = _ffn(x, mrows, per_row, ng[2], w1[l, 1], w3[l, 1], w2[l, 1], 6)
        finals.append(x.reshape(nb, s, D_MODEL))

    return (finals[0], finals[1],
            jnp.stack(ctx_out['diff_k'], axis=1), jnp.stack(ctx_out['diff_v'], axis=1),
            jnp.stack(ctx_out['state'], axis=1),
            jnp.stack(ctx_out['win_k'], axis=1), jnp.stack(ctx_out['win_v'], axis=1))
```

```python
import functools
import math

import numpy as np
import jax
import jax.numpy as jnp
from jax import lax
from jax.experimental import pallas as pl
from jax.experimental.pallas import tpu as pltpu

F32 = jnp.float32
BF16 = jnp.bfloat16

D_MODEL = 1024
DEPTH = 2
N_MOD = 9
D_FF = 2816
GRID_W = 64
ROPE_THETA = 10000.0
EPS = 1e-6
NEG_INF = -1e30

D_RNN = 512
RNN_BLOCKS = 8
RNN_BW = D_RNN // RNN_BLOCKS
LRU_C = 8.0

DIFF_HEADS = 4
DIFF_HD = 64
DIFF_W = DIFF_HEADS * 2 * DIFF_HD

WIN_HEADS = 16
WIN_KV = 4
WIN_G = WIN_HEADS // WIN_KV
WIN_HD = 64
WINDOW = 128

EVEN_IN = 2 * D_RNN + 3 * DIFF_W
ODD_IN = (WIN_HEADS + 2 * WIN_KV) * WIN_HD
ODD_MIX = WIN_HEADS * WIN_HD

LANES = 128
SUBLANES = 8
HEAD_HALF = LANES // 2
ROPE_PAIR = DIFF_HD // 4
ONES_ROWS = 16

TOKEN_TILE = 512
FF_CHUNK = 256
MOD_COLS = 1152
COND_ROWS = 16
LRU_CHUNK = 256
DIFF_Q_TILE = 256
WIN_Q_TILE = 128
VMEM_LIMIT = 56 * 1024 * 1024


def _cparams(*sem):
    return pltpu.CompilerParams(dimension_semantics=sem, vmem_limit_bytes=VMEM_LIMIT)


def _resident(shape):
    return pl.BlockSpec(shape, lambda *_: (0,) * len(shape), pipeline_mode=pl.Buffered(1))


def _dot(a, b):
    return jnp.dot(a, b, preferred_element_type=F32)


def _dot_nt(a, b):
    return lax.dot_general(a, b, (((1,), (1,)), ((), ())), preferred_element_type=F32)


def _rms_mod(x, ng, shift, scale):
    ms = jnp.mean(x * x, axis=-1, keepdims=True)
    y = x * lax.rsqrt(ms + EPS) * ng
    return y * (1.0 + scale) + shift


def _silu(a):
    return a * jax.nn.sigmoid(a)


def _lo_mask():
    return lax.broadcasted_iota(jnp.int32, (1, LANES), 1) < HEAD_HALF


def _head_norm(t, g):
    lo = _lo_mask()
    sq = t * t
    s_lo = jnp.sum(jnp.where(lo, sq, 0.0), axis=-1, keepdims=True)
    s_hi = jnp.sum(jnp.where(lo, 0.0, sq), axis=-1, keepdims=True)
    inv = jnp.where(lo, lax.rsqrt(s_lo / DIFF_HD + EPS), lax.rsqrt(s_hi / DIFF_HD + EPS))
    return t * inv * g


def _rope(t, cos, sin_signed):
    lane = lax.broadcasted_iota(jnp.int32, (1, LANES), 1)
    first = (lane % (2 * ROPE_PAIR)) < ROPE_PAIR
    partner = jnp.where(first, pltpu.roll(t, LANES - ROPE_PAIR, axis=1), pltpu.roll(t, ROPE_PAIR, axis=1))
    return t * cos + partner * sin_signed


def _swap_halves(t):
    return pltpu.roll(t, HEAD_HALF, axis=1)


def _dup_half(t, which):
    lo = _lo_mask()
    r = _swap_halves(t)
    return jnp.where(lo, t, r) if which == 0 else jnp.where(lo, r, t)


def _mod_kernel(c_ref, w_ref, b_ref, o_ref):
    c = c_ref[...]
    s = _silu(c).astype(BF16)
    o_ref[...] = _dot(s, w_ref[...].astype(BF16)) + b_ref[...]


def _modulation(cond, w_mod, b_mod):
    n_col = N_MOD * D_MODEL
    return pl.pallas_call(
        _mod_kernel,
        grid=(DEPTH, n_col // MOD_COLS),
        in_specs=[
            pl.BlockSpec((COND_ROWS, D_MODEL), lambda l, j: (0, 0)),
            pl.BlockSpec((None, D_MODEL, MOD_COLS), lambda l, j: (l, 0, j)),
            pl.BlockSpec((None, 1, MOD_COLS), lambda l, j: (l, 0, j)),
        ],
        out_specs=pl.BlockSpec((None, COND_ROWS, MOD_COLS), lambda l, j: (l, 0, j)),
        out_shape=jax.ShapeDtypeStruct((DEPTH, COND_ROWS, n_col), F32),
        compiler_params=_cparams("arbitrary", "arbitrary"),
        name="modulation",
    )(cond, w_mod, b_mod.reshape(DEPTH, 1, n_col))


def _tok_spec(width, tm):
    return pl.BlockSpec((tm, width), lambda i: (i, 0))


def _mod_spec(tm, tokens_per_row):
    return pl.BlockSpec((None, N_MOD, D_MODEL), lambda i: ((i * tm) // tokens_per_row, 0, 0))


def _ffn_kernel(x_ref, m_ref, ng_ref, w1_ref, w3_ref, w2_ref, o_ref, g_scr, *, mi):
    x = x_ref[...]
    h = _rms_mod(x, ng_ref[...], m_ref[mi:mi + 1, :], m_ref[mi + 1:mi + 2, :]).astype(BF16)
    for j in range(D_FF // FF_CHUNK):
        cols = slice(j * FF_CHUNK, (j + 1) * FF_CHUNK)
        a = _dot(h, w1_ref[:, cols])
        b = _dot(h, w3_ref[:, cols])
        g_scr[:, cols] = (_silu(a) * b).astype(BF16)
    y = _dot(g_scr[...], w2_ref[...])
    o_ref[...] = x + 0.5 * m_ref[mi + 2:mi + 3, :] * y


def _ffn(x2d, mrows, tokens_per_row, ng, w1, w3, w2, mi):
    n = x2d.shape[0]
    tm = TOKEN_TILE
    return pl.pallas_call(
        functools.partial(_ffn_kernel, mi=mi),
        grid=(n // tm,),
        in_specs=[
            _tok_spec(D_MODEL, tm),
            _mod_spec(tm, tokens_per_row),
            _resident((1, D_MODEL)),
            _resident((D_MODEL, D_FF)),
            _resident((D_MODEL, D_FF)),
            _resident((D_FF, D_MODEL)),
        ],
        out_specs=_tok_spec(D_MODEL, tm),
        out_shape=jax.ShapeDtypeStruct((n, D_MODEL), F32),
        scratch_shapes=[pltpu.VMEM((tm, D_FF), BF16)],
        compiler_params=_cparams("arbitrary"),
        name="swiglu",
    )(x2d, mrows, ng, w1, w3, w2)


def _out_kernel(*refs, n_act):
    x_ref, m_ref = refs[0], refs[1]
    acts = refs[2:2 + n_act]
    w_ref, o_ref = refs[2 + n_act], refs[3 + n_act]
    y = None
    row = 0
    for a_ref in acts:
        ka = a_ref.shape[1]
        part = _dot(a_ref[...], w_ref[row:row + ka, :])
        y = part if y is None else y + part
        row += ka
    o_ref[...] = x_ref[...] + m_ref[5:6, :] * y


def _mixer_out(x2d, mrows, tokens_per_row, acts, w_out):
    n = x2d.shape[0]
    tm = TOKEN_TILE
    return pl.pallas_call(
        functools.partial(_out_kernel, n_act=len(acts)),
        grid=(n // tm,),
        in_specs=[_tok_spec(D_MODEL, tm), _mod_spec(tm, tokens_per_row)]
        + [_tok_spec(a.shape[1], tm) for a in acts]
        + [_resident(w_out.shape)],
        out_specs=_tok_spec(D_MODEL, tm),
        out_shape=jax.ShapeDtypeStruct((n, D_MODEL), F32),
        compiler_params=_cparams("arbitrary"),
        name="mixer_out",
    )(x2d, mrows, *acts, w_out)


def _even_in_kernel(*refs, latent):
    if latent:
        x_ref, m_ref, ng_ref, w_ref, qg_ref, kg_ref, cos_ref, sin_ref = refs[:8]
        outs = refs[8:]
    else:
        x_ref, m_ref, ng_ref, w_ref, qg_ref, kg_ref = refs[:6]
        outs = refs[6:]
    xr_ref, gt_ref, q_ref, k_ref, v_ref = outs
    h = _rms_mod(x_ref[...], ng_ref[...], m_ref[3:4, :], m_ref[4:5, :]).astype(BF16)
    xr_ref[...] = _dot(h, w_ref[:, 0:D_RNN])
    gt_ref[...] = _dot(h, w_ref[:, D_RNN:2 * D_RNN])
    base = 2 * D_RNN
    q = _dot(h, w_ref[:, base:base + DIFF_W])
    k = _dot(h, w_ref[:, base + DIFF_W:base + 2 * DIFF_W])
    v_ref[...] = _dot(h, w_ref[:, base + 2 * DIFF_W:base + 3 * DIFF_W]).astype(v_ref.dtype)
    for hh in range(DIFF_HEADS):
        cols = slice(hh * LANES, (hh + 1) * LANES)
        qh = _head_norm(q[:, cols], qg_ref[...])
        kh = _head_norm(k[:, cols], kg_ref[...])
        if latent:
            qh = _rope(qh, cos_ref[...], sin_ref[...])
            kh = _rope(kh, cos_ref[...], sin_ref[...])
        q_ref[:, cols] = (qh * (DIFF_HD ** -0.5)).astype(BF16)
        k_ref[:, cols] = kh.astype(k_ref.dtype)


def _even_in(x2d, mrows, tokens_per_row, ng, w_in, qg, kg, rope):
    n = x2d.shape[0]
    tm = TOKEN_TILE
    latent = rope is not None
    in_specs = [
        _tok_spec(D_MODEL, tm), _mod_spec(tm, tokens_per_row), _resident((1, D_MODEL)),
        _resident((D_MODEL, EVEN_IN)), _resident((1, LANES)), _resident((1, LANES)),
    ]
    args = [x2d, mrows, ng, w_in, qg, kg]
    if latent:
        seq = rope[0].shape[0]
        tab = pl.BlockSpec((tm, LANES), lambda i: (i % (seq // tm), 0))
        in_specs += [tab, tab]
        args += list(rope)
    kv_dtype = BF16 if latent else F32
    return pl.pallas_call(
        functools.partial(_even_in_kernel, latent=latent),
        grid=(n // tm,),
        in_specs=in_specs,
        out_specs=[_tok_spec(D_RNN, tm)] * 2 + [_tok_spec(DIFF_W, tm)] * 3,
        out_shape=[
            jax.ShapeDtypeStruct((n, D_RNN), F32), jax.ShapeDtypeStruct((n, D_RNN), F32),
            jax.ShapeDtypeStruct((n, DIFF_W), BF16),
            jax.ShapeDtypeStruct((n, DIFF_W), kv_dtype), jax.ShapeDtypeStruct((n, DIFF_W), kv_dtype),
        ],
        compiler_params=_cparams("arbitrary"),
        name="even_in",
    )(*args)


def _gelu_tanh(x):
    return 0.5 * x * (1.0 + jnp.tanh(math.sqrt(2.0 / math.pi) * (x + 0.044715 * (x * x * x))))


def _group_scan(a, u, reverse):
    t, c = a.shape
    a3 = a.reshape(t // SUBLANES, SUBLANES, c)
    b3 = u.reshape(t // SUBLANES, SUBLANES, c)
    row = lax.broadcasted_iota(jnp.int32, (1, SUBLANES, c), 1)
    d = 1
    while d < SUBLANES:
        if reverse:
            keep = row < SUBLANES - d
            shift = SUBLANES - d
        else:
            keep = row >= d
            shift = d
        a_prev = jnp.where(keep, pltpu.roll(a3, shift, axis=1), 1.0)
        b_prev = jnp.where(keep, pltpu.roll(b3, shift, axis=1), 0.0)
        b3 = a3 * b_prev + b3
        a3 = a3 * a_prev
        d *= 2
    return a3.reshape(t, c), b3.reshape(t, c)


def _lru_kernel(xr_ref, gt_ref, cw_ref, cb_ref, wg_ref, bg_ref, lam_ref, h0_ref,
                y_ref, last_ref, xc_scr, hf_scr, a_scr, b_scr, *, seq, tc):
    n_chunks = seq // tc
    n_groups = tc // SUBLANES
    lam = lam_ref[...]
    sp = jnp.maximum(-lam, 0.0) + jnp.log1p(jnp.exp(-jnp.abs(lam)))

    def conv_chunk(c):
        r0 = pl.multiple_of(c * tc, tc)
        p0 = pl.multiple_of(jnp.maximum(r0 - SUBLANES, 0), SUBLANES)
        n0 = pl.multiple_of(jnp.minimum(r0 + tc, seq - SUBLANES), SUBLANES)
        prev = jnp.where(c > 0, xr_ref[pl.ds(p0, SUBLANES), :], 0.0)
        nxt = jnp.where(c < n_chunks - 1, xr_ref[pl.ds(n0, SUBLANES), :], 0.0)
        xw = jnp.concatenate([prev, xr_ref[pl.ds(r0, tc), :], nxt], axis=0)
        xc = cb_ref[...]
        for tap in range(4):
            off = SUBLANES - 2 + tap
            xc = xc + xw[off:off + tc, :] * cw_ref[tap:tap + 1, :]
        return xc

    def gates(xc, d):
        cols = slice(d * 2 * D_RNN, (d + 1) * 2 * D_RNN)
        z = _dot(xc.astype(BF16), wg_ref[:, cols]) + bg_ref[:, cols]
        r = jax.nn.sigmoid(z[:, :D_RNN])
        i = jax.nn.sigmoid(z[:, D_RNN:])
        log_a = -LRU_C * r * sp[d:d + 1, :]
        a = jnp.exp(log_a)
        th = jnp.tanh(log_a)
        u = jnp.sqrt(-2.0 * th / (1.0 - th)) * (i * xc)
        return a, u

    def carry_loop(carry, dst_ref, dst_base, reverse):
        def body(g, carry):
            gi = (n_groups - 1 - g) if reverse else g
            r0 = pl.multiple_of(gi * SUBLANES, SUBLANES)
            h8 = a_scr[pl.ds(r0, SUBLANES), :] * carry + b_scr[pl.ds(r0, SUBLANES), :]
            dst_ref[pl.ds(pl.multiple_of(dst_base + r0, SUBLANES), SUBLANES), :] = h8
            return h8[0:1, :] if reverse else h8[SUBLANES - 1:SUBLANES, :]
        return lax.fori_loop(0, n_groups, body, carry, unroll=4)

    def fwd_chunk(c, carry):
        r0 = pl.multiple_of(c * tc, tc)
        xc = conv_chunk(c)
        xc_scr[pl.ds(r0, tc), :] = xc
        a, u = gates(xc, 0)
        a_cum, b_cum = _group_scan(a, u, False)
        a_scr[...] = a_cum
        b_scr[...] = b_cum
        return carry_loop(carry, hf_scr, r0, False)

    def bwd_chunk(cc, carry):
        c = n_chunks - 1 - cc
        r0 = pl.multiple_of(c * tc, tc)
        a, u = gates(xc_scr[pl.ds(r0, tc), :], 1)
        a_cum, b_cum = _group_scan(a, u, True)
        a_scr[...] = a_cum
        b_scr[...] = b_cum
        carry = carry_loop(carry, b_scr, 0, True)
        hsum = hf_scr[pl.ds(r0, tc), :] + b_scr[...]
        y_ref[pl.ds(r0, tc), :] = (hsum * _gelu_tanh(gt_ref[pl.ds(r0, tc), :])).astype(y_ref.dtype)
        return carry

    last_f = lax.fori_loop(0, n_chunks, fwd_chunk, h0_ref[0:1, :])
    last_b = lax.fori_loop(0, n_chunks, bwd_chunk, h0_ref[1:2, :])
    last_ref[0:1, :] = last_f
    last_ref[1:2, :] = last_b


def _lru(xr, gt, conv_w, conv_b, wg, bg, lam, h0):
    b, seq, _ = xr.shape
    tc = LRU_CHUNK
    seq_spec = pl.BlockSpec((None, seq, D_RNN), lambda i: (i, 0, 0))
    st_spec = pl.BlockSpec((None, 2, D_RNN), lambda i: (i, 0, 0))
    return pl.pallas_call(
        functools.partial(_lru_kernel, seq=seq, tc=tc),
        grid=(b,),
        in_specs=[seq_spec, seq_spec, _resident((4, D_RNN)), _resident((1, D_RNN)),
                  _resident((D_RNN, 4 * D_RNN)), _resident((1, 4 * D_RNN)), _resident((2, D_RNN)), st_spec],
        out_specs=[seq_spec, st_spec],
        out_shape=[jax.ShapeDtypeStruct((b, seq, D_RNN), BF16), jax.ShapeDtypeStruct((b, 2, D_RNN), F32)],
        scratch_shapes=[pltpu.VMEM((seq, D_RNN), F32), pltpu.VMEM((seq, D_RNN), F32),
                        pltpu.VMEM((tc, D_RNN), F32), pltpu.VMEM((tc, D_RNN), F32)],
        compiler_params=_cparams("arbitrary"),
        name="rglru",
    )(xr, gt, conv_w, conv_b, wg, bg, lam, h0)


def _diff_attn_kernel(*refs, n_src, lam_init):
    q_ref, lamv_ref, sg_ref = refs[:3]
    srcs = [(refs[3 + 2 * s], refs[4 + 2 * s]) for s in range(n_src)]
    o_ref = refs[3 + 2 * n_src]
    scr = refs[4 + 2 * n_src:]
    kb = [scr[2 * s] for s in range(n_src)]
    vt = [scr[2 * s + 1] for s in range(n_src)]

    @pl.when(pl.program_id(1) == 0)
    def _():
        for s, (k_ref, v_ref) in enumerate(srcs):
            kb[s][...] = k_ref[...].astype(BF16)
            for hh in range(DIFF_HEADS):
                cols = slice(hh * LANES, (hh + 1) * LANES)
                vt[s][hh, 0:LANES, :] = v_ref[:, cols].astype(F32).T.astype(BF16)
                vt[s][hh, LANES:, :] = jnp.ones((ONES_ROWS, v_ref.shape[0]), BF16)

    lv = lamv_ref[...]
    lam = (jnp.exp(jnp.sum(lv[0:1, :] * lv[1:2, :], axis=-1, keepdims=True))
           - jnp.exp(jnp.sum(lv[2:3, :] * lv[3:4, :], axis=-1, keepdims=True)) + lam_init)
    lo = _lo_mask()
    zero = jnp.zeros((), BF16)
    tq = q_ref.shape[0]

    def scores(hh):
        cols = slice(hh * LANES, (hh + 1) * LANES)
        qh = q_ref[:, cols]
        qs = jnp.concatenate([jnp.where(lo, qh, zero), jnp.where(lo, zero, qh)], axis=0)
        return [_dot_nt(kb[s][:, cols], qs) for s in range(n_src)]

    sts_next = scores(0)
    for hh in range(DIFF_HEADS):
        cols = slice(hh * LANES, (hh + 1) * LANES)
        sts = sts_next
        if hh + 1 < DIFF_HEADS:
            sts_next = scores(hh + 1)
        m = functools.reduce(jnp.maximum, [jnp.max(st, axis=0, keepdims=True) for st in sts])
        ps = [jnp.exp(st - m) for st in sts]
        o2t = functools.reduce(jnp.add, [_dot(vt[s][hh], ps[s].astype(BF16)) for s in range(n_src)])
        o2t = o2t[0:LANES, :] * (1.0 / o2t[LANES:LANES + 1, :])
        o = (o2t[:, :tq] - lam * o2t[:, tq:]).T
        ms = jnp.mean(o * o, axis=-1, keepdims=True)
        o = o * lax.rsqrt(ms + EPS) * sg_ref[...] * (1.0 - lam_init)
        o_ref[:, cols] = o.astype(o_ref.dtype)


def _diff_attn(q, lam_vec, subln_g, srcs, lam_init):
    b, seq, _ = q.shape
    tq = min(DIFF_Q_TILE, seq)
    q_spec = pl.BlockSpec((None, tq, DIFF_W), lambda i, j: (i, j, 0))
    in_specs = [q_spec, _resident((4, DIFF_HD)), _resident((1, LANES))]
    args = [q, lam_vec, subln_g]
    scratch = []
    for k, v in srcs:
        kv_spec = pl.BlockSpec((None, k.shape[1], DIFF_W), lambda i, j: (i, 0, 0))
        in_specs += [kv_spec, kv_spec]
        args += [k, v]
        scratch += [pltpu.VMEM((k.shape[1], DIFF_W), BF16),
                    pltpu.VMEM((DIFF_HEADS, LANES + ONES_ROWS, k.shape[1]), BF16)]
    return pl.pallas_call(
        functools.partial(_diff_attn_kernel, n_src=len(srcs), lam_init=lam_init),
        grid=(b, seq // tq),
        in_specs=in_specs,
        out_specs=q_spec,
        out_shape=jax.ShapeDtypeStruct((b, seq, DIFF_W), BF16),
        scratch_shapes=scratch,
        compiler_params=_cparams("arbitrary", "arbitrary"),
        name="diff_attn",
    )(*args)


def _odd_in_kernel(*refs, latent):
    if latent:
        x_ref, m_ref, ng_ref, w_ref, qg_ref, kg_ref, cos_ref, sin_ref = refs[:8]
        q_ref, kd_ref, v_ref = refs[8:]
    else:
        x_ref, m_ref, ng_ref, w_ref, qg_ref, kg_ref = refs[:6]
        q_ref, kd_ref, k_ref, v_ref = refs[6:]
    h = _rms_mod(x_ref[...], ng_ref[...], m_ref[3:4, :], m_ref[4:5, :]).astype(BF16)
    n_q = WIN_HEADS * WIN_HD
    n_kv = WIN_KV * WIN_HD
    q = _dot(h, w_ref[:, 0:n_q])
    for blk in range(n_q // LANES):
        cols = slice(blk * LANES, (blk + 1) * LANES)
        qh = _head_norm(q[:, cols], qg_ref[...])
        if latent:
            qh = _rope(qh, cos_ref[...], sin_ref[...])
        q_ref[:, cols] = (qh * (WIN_HD ** -0.5)).astype(BF16)
    k = _dot(h, w_ref[:, n_q:n_q + n_kv])
    v = _dot(h, w_ref[:, n_q + n_kv:n_q + 2 * n_kv])
    for blk in range(n_kv // LANES):
        cols = slice(blk * LANES, (blk + 1) * LANES)
        kh = _head_norm(k[:, cols], kg_ref[...])
        if latent:
            kh = _rope(kh, cos_ref[...], sin_ref[...])
        else:
            k_ref[:, cols] = kh
        for half in range(2):
            dst = slice((2 * blk + half) * LANES, (2 * blk + half + 1) * LANES)
            kd_ref[:, dst] = _dup_half(kh, half).astype(BF16)
    v_ref[...] = v.astype(v_ref.dtype)


def _odd_in(x2d, mrows, tokens_per_row, ng, w_in, qg, kg, rope):
    n = x2d.shape[0]
    tm = TOKEN_TILE
    latent = rope is not None
    n_q = WIN_HEADS * WIN_HD
    n_kv = WIN_KV * WIN_HD
    in_specs = [
        _tok_spec(D_MODEL, tm), _mod_spec(tm, tokens_per_row), _resident((1, D_MODEL)),
        _resident((D_MODEL, ODD_IN)), _resident((1, LANES)), _resident((1, LANES)),
    ]
    args = [x2d, mrows, ng, w_in, qg, kg]
    out_specs = [_tok_spec(n_q, tm), _tok_spec(2 * n_kv, tm)]
    out_shape = [jax.ShapeDtypeStruct((n, n_q), BF16), jax.ShapeDtypeStruct((n, 2 * n_kv), BF16)]
    if latent:
        seq = rope[0].shape[0]
        tab = pl.BlockSpec((tm, LANES), lambda i: (i % (seq // tm), 0))
        in_specs += [tab, tab]
        args += list(rope)
        out_specs += [_tok_spec(n_kv, tm)]
        out_shape += [jax.ShapeDtypeStruct((n, n_kv), BF16)]
    else:
        out_specs += [_tok_spec(n_kv, tm)] * 2
        out_shape += [jax.ShapeDtypeStruct((n, n_kv), F32)] * 2
    return pl.pallas_call(
        functools.partial(_odd_in_kernel, latent=latent),
        grid=(n // tm,),
        in_specs=in_specs,
        out_specs=out_specs,
        out_shape=out_shape,
        compiler_params=_cparams("arbitrary"),
        name="odd_in",
    )(*args)


def _win_attn_kernel(*refs, latent, seq, tq):
    if latent:
        q_ref, sink_ref, kd_ref, v_ref, ck_ref, cv_ref, o_ref, vt_scr, ckd_scr, cvt_scr = refs
    else:
        q_ref, sink_ref, kd_ref, v_ref, o_ref, vt_scr = refs
    lo = _lo_mask()
    zero = jnp.zeros((), BF16)
    n_cols = WIN_G * tq
    n_kv = WIN_KV * WIN_HD

    @pl.when(pl.program_id(1) == 0)
    def _():
        srcs = [(v_ref, vt_scr)] + ([(cv_ref, cvt_scr)] if latent else [])
        for src_ref, dst_scr in srcs:
            for blk in range(n_kv // LANES):
                vt2 = src_ref[:, blk * LANES:(blk + 1) * LANES].astype(F32).T.astype(BF16)
                for half in range(2):
                    dst_scr[2 * blk + half, 0:WIN_HD, :] = vt2[half * WIN_HD:(half + 1) * WIN_HD, :]
            for j in range(WIN_KV):
                dst_scr[j, WIN_HD:, :] = jnp.ones((ONES_ROWS, src_ref.shape[0]), BF16)
        if latent:
            for j in range(WIN_KV):
                cblk = slice((j // 2) * LANES, (j // 2 + 1) * LANES)
                ckd_scr[:, j * LANES:(j + 1) * LANES] = _dup_half(ck_ref[:, cblk], j % 2).astype(BF16)

    if latent:
        span = 3 * tq
        i = pl.program_id(1)
        start = pl.multiple_of(jnp.clip((i - 1) * tq, 0, seq - span), tq)
        kpos = start + lax.broadcasted_iota(jnp.int32, (span, 1), 0)
        qpos = i * tq + (lax.broadcasted_iota(jnp.int32, (1, n_cols), 1) & (tq - 1))
        bias = jnp.where(jnp.abs(kpos - qpos) <= WINDOW, 0.0, NEG_INF)

    def scores(j):
        kcols = slice(j * LANES, (j + 1) * LANES)
        qs = []
        for pair in range(WIN_G // 2):
            qb = q_ref[:, (2 * j + pair) * LANES:(2 * j + pair + 1) * LANES]
            qs += [jnp.where(lo, qb, zero), jnp.where(lo, zero, qb)]
        qs = jnp.concatenate(qs, axis=0)
        if latent:
            return [_dot_nt(kd_ref[pl.ds(start, span), kcols], qs) + bias, _dot_nt(ckd_scr[:, kcols], qs)]
        return [_dot_nt(kd_ref[:, kcols], qs)]

    sts_next = scores(0)
    for j in range(WIN_KV):
        sts = sts_next
        if j + 1 < WIN_KV:
            sts_next = scores(j + 1)
        sink = jnp.concatenate([jnp.full((1, tq), sink_ref[WIN_G * j + g], F32) for g in range(WIN_G)], axis=1)
        vts = [vt_scr[j, :, pl.ds(start, span)], cvt_scr[j]] if latent else [vt_scr[j]]
        m = functools.reduce(jnp.maximum, [jnp.max(st, axis=0, keepdims=True) for st in sts])
        m = jnp.maximum(m, sink)
        ps = [jnp.exp(st - m) for st in sts]
        ot = functools.reduce(jnp.add, [_dot(vt, p.astype(BF16)) for vt, p in zip(vts, ps)])
        den = ot[WIN_HD:WIN_HD + 1, :] + jnp.exp(sink - m)
        ot = ot[0:WIN_HD, :] * (1.0 / den)
        for pair in range(WIN_G // 2):
            blk = 2 * j + pair
            c0 = 2 * pair * tq
            both = jnp.concatenate([ot[:, c0:c0 + tq], ot[:, c0 + tq:c0 + 2 * tq]], axis=0)
            o_ref[:, blk * LANES:(blk + 1) * LANES] = both.T.astype(o_ref.dtype)


def _win_attn(q, sink, kd, v, ctx):
    b, seq, n_q = q.shape
    n_kv = v.shape[2]
    latent = ctx is not None
    tq = WIN_Q_TILE if latent else min(seq, 2 * WIN_Q_TILE)
    q_spec = pl.BlockSpec((None, tq, n_q), lambda i, j: (i, j, 0))
    kd_spec = pl.BlockSpec((None, seq, kd.shape[2]), lambda i, j: (i, 0, 0))
    v_spec = pl.BlockSpec((None, seq, n_kv), lambda i, j: (i, 0, 0))
    in_specs = [q_spec, pl.BlockSpec(memory_space=pltpu.SMEM), kd_spec, v_spec]
    args = [q, sink, kd, v]
    scratch = [pltpu.VMEM((WIN_KV, WIN_HD + ONES_ROWS, seq), BF16)]
    if latent:
        ck, cv = ctx
        past = ck.shape[1]
        c_spec = pl.BlockSpec((None, past, n_kv), lambda i, j: (i, 0, 0))
        in_specs += [c_spec, c_spec]
        args += [ck, cv]
        scratch += [pltpu.VMEM((past, kd.shape[2]), BF16), pltpu.VMEM((WIN_KV, WIN_HD + ONES_ROWS, past), BF16)]
    return pl.pallas_call(
        functools.partial(_win_attn_kernel, latent=latent, seq=seq, tq=tq),
        grid=(b, seq // tq),
        in_specs=in_specs,
        out_specs=q_spec,
        out_shape=jax.ShapeDtypeStruct((b, seq, n_q), BF16),
        scratch_shapes=scratch,
        compiler_params=_cparams("arbitrary", "arbitrary"),
        name="win_attn",
    )(*args)


def _rope_tables(seq):
    rows = seq // GRID_W
    row = np.repeat(np.arange(rows), GRID_W).astype(np.float32)
    col = np.tile(np.arange(GRID_W), rows).astype(np.float32)
    half = DIFF_HD // 2
    inv = (ROPE_THETA ** (-np.arange(0, half, 2, dtype=np.float32) / half)).astype(np.float32)
    ang_r = row[:, None] * inv[None, :]
    ang_c = col[:, None] * inv[None, :]
    cos = np.concatenate([np.cos(ang_r), np.cos(ang_r), np.cos(ang_c), np.cos(ang_c)], axis=1)
    sin = np.concatenate([-np.sin(ang_r), np.sin(ang_r), -np.sin(ang_c), np.sin(ang_c)], axis=1)
    cos = np.tile(cos, (1, LANES // DIFF_HD)).astype(np.float32)
    sin = np.tile(sin, (1, LANES // DIFF_HD)).astype(np.float32)
    return jnp.asarray(cos), jnp.asarray(sin)


def _block_diag(w):
    eye = jnp.eye(RNN_BLOCKS, dtype=w.dtype)
    return jnp.einsum('nkj,nm->nkmj', w, eye).reshape(D_RNN, D_RNN)


def _tile_gain(g):
    return jnp.tile(g, LANES // g.shape[0]).reshape(1, LANES)


def _diff_lambda_init(layer):
    return 0.8 - 0.6 * math.exp(-0.3 * layer)


def kernel(x_prompt, x_sample, cache_diff_k, cache_diff_v, state_lru, cache_win_k, cache_win_v, c, c_ctx,
           norm_g, w_mod, b_mod, ffn_w1, ffn_w3, ffn_w2, e_w_in, e_w_out, e_conv_w, e_conv_b,
           e_lru_wa, e_lru_ba, e_lru_wi, e_lru_bi, e_lru_lam, e_q_g, e_k_g, e_lam, e_subln_g,
           o_w_in, o_w_out, o_q_g, o_k_g, o_sink):
    batch, seq, _ = x_prompt.shape
    dec_batch, dec_seq, _ = x_sample.shape
    past = cache_diff_k.shape[2]

    cond = jnp.concatenate([c_ctx[None, :], c], axis=0)
    cond = jnp.pad(cond, ((0, COND_ROWS - cond.shape[0]), (0, 0)))
    mod = _modulation(cond, w_mod, b_mod).reshape(DEPTH, COND_ROWS, N_MOD, D_MODEL)

    w1 = ffn_w1.astype(BF16)
    w3 = ffn_w3.astype(BF16)
    w2 = ffn_w2.astype(BF16)
    rope = _rope_tables(dec_seq)

    groups = [
        dict(x=x_prompt.reshape(batch * seq, D_MODEL), b=batch, s=seq, latent=False,
             rows=slice(0, 1), per_row=batch * seq),
        dict(x=x_sample.reshape(dec_batch * dec_seq, D_MODEL), b=dec_batch, s=dec_seq, latent=True,
             rows=slice(1, 1 + dec_batch), per_row=dec_seq),
    ]
    ctx_out = {}
    finals = []
    for grp in groups:
        x = grp['x']
        nb, s, latent, per_row = grp['b'], grp['s'], grp['latent'], grp['per_row']
        for l in range(DEPTH):
            j = l // 2
            mrows = mod[l, grp['rows']]
            ng = norm_g[l].reshape(3, 1, D_MODEL)
            x = _ffn(x, mrows, per_row, ng[0], w1[l, 0], w3[l, 0], w2[l, 0], 0)
            if l % 2 == 0:
                xr, gt, q, k, v = _even_in(x, mrows, per_row, ng[1], e_w_in[j].astype(BF16),
                                           _tile_gain(e_q_g[j]), _tile_gain(e_k_g[j]), rope if latent else None)
                wg = jnp.concatenate([_block_diag(e_lru_wa[j, 0]), _block_diag(e_lru_wi[j, 0]),
                                      _block_diag(e_lru_wa[j, 1]), _block_diag(e_lru_wi[j, 1])], axis=1).astype(BF16)
                bg = jnp.concatenate([e_lru_ba[j, 0], e_lru_bi[j, 0], e_lru_ba[j, 1], e_lru_bi[j, 1]]).reshape(1, -1)
                h0 = state_lru[:, j] if latent else jnp.zeros((nb, 2, D_RNN), F32)
                y_rnn, last = _lru(xr.reshape(nb, s, D_RNN), gt.reshape(nb, s, D_RNN), e_conv_w[j],
                                   e_conv_b[j].reshape(1, D_RNN), wg, bg, e_lru_lam[j], h0)
                k3 = k.reshape(nb, s, DIFF_W)
                v3 = v.reshape(nb, s, DIFF_W)
                srcs = [(k3, v3)]
                if latent:
                    srcs = [(cache_diff_k[:, j].reshape(nb, past, DIFF_W),
                             cache_diff_v[:, j].reshape(nb, past, DIFF_W))] + srcs
                else:
                    ctx_out.setdefault('diff_k', []).append(k3.reshape(nb, s, DIFF_HEADS, 2 * DIFF_HD))
                    ctx_out.setdefault('diff_v', []).append(v3.reshape(nb, s, DIFF_HEADS, 2 * DIFF_HD))
                    ctx_out.setdefault('state', []).append(last)
                o = _diff_attn(q.reshape(nb, s, DIFF_W), e_lam[j], e_subln_g[j].reshape(1, LANES), srcs,
                               _diff_lambda_init(l))
                acts = [y_rnn.reshape(nb * s, D_RNN), o.reshape(nb * s, DIFF_W)]
                x = _mixer_out(x, mrows, per_row, acts, e_w_out[j].astype(BF16))
            else:
                outs = _odd_in(x, mrows, per_row, ng[1], o_w_in[j].astype(BF16),
                               _tile_gain(o_q_g[j]), _tile_gain(o_k_g[j]), rope if latent else None)
                q, kd = outs[:2]
                v = outs[-1]
                n_kv = WIN_KV * WIN_HD
                ctx = None
                if latent:
                    ctx = (cache_win_k[:, j].reshape(nb, past, n_kv), cache_win_v[:, j].reshape(nb, past, n_kv))
                else:
                    ctx_out.setdefault('win_k', []).append(outs[2].reshape(nb, s, WIN_KV, WIN_HD))
                    ctx_out.setdefault('win_v', []).append(v.reshape(nb, s, WIN_KV, WIN_HD))
                o = _win_attn(q.reshape(nb, s, ODD_MIX), o_sink[j], kd.reshape(nb, s, 2 * n_kv),
                              v.reshape(nb, s, n_kv), ctx)
                x = _mixer_out(x, mrows, per_row, [o.reshape(nb * s, ODD_MIX)], o_w_out[j].astype(BF16))
            x = _ffn(x, mrows, per_row, ng[2], w1[l, 1], w3[l, 1], w2[l, 1], 6)
        finals.append(x.reshape(nb, s, D_MODEL))

    return (finals[0], finals[1],
            jnp.stack(ctx_out['diff_k'], axis=1), jnp.stack(ctx_out['diff_v'], axis=1),
            jnp.stack(ctx_out['state'], axis=1),
            jnp.stack(ctx_out['win_k'], axis=1), jnp.stack(ctx_out['win_v'], axis=1))
```

```python
import functools
import math

import numpy as np
import jax
import jax.numpy as jnp
from jax import lax
from jax.experimental import pallas as pl
from jax.experimental.pallas import tpu as pltpu

F32 = jnp.float32
BF16 = jnp.bfloat16

D_MODEL = 1024
DEPTH = 2
N_MOD = 9
D_FF = 2816
GRID_W = 64
ROPE_THETA = 10000.0
EPS = 1e-6
NEG_INF = -1e30

D_RNN = 512
RNN_BLOCKS = 8
RNN_BW = D_RNN // RNN_BLOCKS
LRU_C = 8.0

DIFF_HEADS = 4
DIFF_HD = 64
DIFF_W = DIFF_HEADS * 2 * DIFF_HD

WIN_HEADS = 16
WIN_KV = 4
WIN_G = WIN_HEADS // WIN_KV
WIN_HD = 64
WINDOW = 128

EVEN_IN = 2 * D_RNN + 3 * DIFF_W
ODD_IN = (WIN_HEADS + 2 * WIN_KV) * WIN_HD
ODD_MIX = WIN_HEADS * WIN_HD

LANES = 128
SUBLANES = 8
HEAD_HALF = LANES // 2
ROPE_PAIR = DIFF_HD // 4
ONES_ROWS = 16

TOKEN_TILE = 512
FF_CHUNK = 256
MOD_COLS = 1152
COND_ROWS = 16
LRU_STEPS = 64
LRU_DOUBLE_BUFFER_MAX = 4 * 1024 * 1024
DIFF_Q_TILE = 256
WIN_Q_TILE = 128
VMEM_LIMIT = 56 * 1024 * 1024


def _cparams(*sem):
    return pltpu.CompilerParams(dimension_semantics=sem, vmem_limit_bytes=VMEM_LIMIT)


def _resident(shape):
    return pl.BlockSpec(shape, lambda *_: (0,) * len(shape), pipeline_mode=pl.Buffered(1))


def _dot(a, b):
    return jnp.dot(a, b, preferred_element_type=F32)


def _dot_nt(a, b):
    return lax.dot_general(a, b, (((1,), (1,)), ((), ())), preferred_element_type=F32)


def _rms_mod(x, ng, shift, scale):
    ms = jnp.mean(x * x, axis=-1, keepdims=True)
    y = x * lax.rsqrt(ms + EPS) * ng
    return y * (1.0 + scale) + shift


def _silu(a):
    return a * jax.nn.sigmoid(a)


def _lo_mask():
    return lax.broadcasted_iota(jnp.int32, (1, LANES), 1) < HEAD_HALF


def _head_norm(t, g):
    lo = _lo_mask()
    sq = t * t
    s_lo = jnp.sum(jnp.where(lo, sq, 0.0), axis=-1, keepdims=True)
    s_hi = jnp.sum(jnp.where(lo, 0.0, sq), axis=-1, keepdims=True)
    inv = jnp.where(lo, lax.rsqrt(s_lo / DIFF_HD + EPS), lax.rsqrt(s_hi / DIFF_HD + EPS))
    return t * inv * g


def _rope(t, cos, sin_signed):
    lane = lax.broadcasted_iota(jnp.int32, (1, LANES), 1)
    first = (lane % (2 * ROPE_PAIR)) < ROPE_PAIR
    partner = jnp.where(first, pltpu.roll(t, LANES - ROPE_PAIR, axis=1), pltpu.roll(t, ROPE_PAIR, axis=1))
    return t * cos + partner * sin_signed


def _swap_halves(t):
    return pltpu.roll(t, HEAD_HALF, axis=1)


def _dup_half(t, which):
    lo = _lo_mask()
    r = _swap_halves(t)
    return jnp.where(lo, t, r) if which == 0 else jnp.where(lo, r, t)


def _mod_kernel(c_ref, w_ref, b_ref, o_ref):
    c = c_ref[...]
    s = _silu(c).astype(BF16)
    o_ref[...] = _dot(s, w_ref[...].astype(BF16)) + b_ref[...]


def _modulation(cond, w_mod, b_mod):
    n_col = N_MOD * D_MODEL
    return pl.pallas_call(
        _mod_kernel,
        grid=(DEPTH, n_col // MOD_COLS),
        in_specs=[
            pl.BlockSpec((COND_ROWS, D_MODEL), lambda l, j: (0, 0)),
            pl.BlockSpec((None, D_MODEL, MOD_COLS), lambda l, j: (l, 0, j)),
            pl.BlockSpec((None, 1, MOD_COLS), lambda l, j: (l, 0, j)),
        ],
        out_specs=pl.BlockSpec((None, COND_ROWS, MOD_COLS), lambda l, j: (l, 0, j)),
        out_shape=jax.ShapeDtypeStruct((DEPTH, COND_ROWS, n_col), F32),
        compiler_params=_cparams("arbitrary", "arbitrary"),
        name="modulation",
    )(cond, w_mod, b_mod.reshape(DEPTH, 1, n_col))


def _tok_spec(width, tm):
    return pl.BlockSpec((tm, width), lambda i: (i, 0))


def _mod_spec(tm, tokens_per_row):
    return pl.BlockSpec((None, N_MOD, D_MODEL), lambda i: ((i * tm) // tokens_per_row, 0, 0))


def _ffn_kernel(x_ref, m_ref, ng_ref, w1_ref, w3_ref, w2_ref, o_ref, g_scr, *, mi):
    x = x_ref[...]
    h = _rms_mod(x, ng_ref[...], m_ref[mi:mi + 1, :], m_ref[mi + 1:mi + 2, :]).astype(BF16)
    for j in range(D_FF // FF_CHUNK):
        cols = slice(j * FF_CHUNK, (j + 1) * FF_CHUNK)
        a = _dot(h, w1_ref[:, cols])
        b = _dot(h, w3_ref[:, cols])
        g_scr[:, cols] = (_silu(a) * b).astype(BF16)
    y = _dot(g_scr[...], w2_ref[...])
    o_ref[...] = x + 0.5 * m_ref[mi + 2:mi + 3, :] * y


def _ffn(x2d, mrows, tokens_per_row, ng, w1, w3, w2, mi):
    n = x2d.shape[0]
    tm = TOKEN_TILE
    return pl.pallas_call(
        functools.partial(_ffn_kernel, mi=mi),
        grid=(n // tm,),
        in_specs=[
            _tok_spec(D_MODEL, tm),
            _mod_spec(tm, tokens_per_row),
            _resident((1, D_MODEL)),
            _resident((D_MODEL, D_FF)),
            _resident((D_MODEL, D_FF)),
            _resident((D_FF, D_MODEL)),
        ],
        out_specs=_tok_spec(D_MODEL, tm),
        out_shape=jax.ShapeDtypeStruct((n, D_MODEL), F32),
        scratch_shapes=[pltpu.VMEM((tm, D_FF), BF16)],
        compiler_params=_cparams("arbitrary"),
        name="swiglu",
    )(x2d, mrows, ng, w1, w3, w2)


def _out_kernel(*refs, n_act):
    x_ref, m_ref = refs[0], refs[1]
    acts = refs[2:2 + n_act]
    w_ref, o_ref = refs[2 + n_act], refs[3 + n_act]
    y = None
    row = 0
    for a_ref in acts:
        ka = a_ref.shape[1]
        part = _dot(a_ref[...].astype(BF16), w_ref[row:row + ka, :])
        y = part if y is None else y + part
        row += ka
    o_ref[...] = x_ref[...] + m_ref[5:6, :] * y


def _mixer_out(x2d, mrows, tokens_per_row, acts, w_out):
    n = x2d.shape[0]
    tm = TOKEN_TILE
    return pl.pallas_call(
        functools.partial(_out_kernel, n_act=len(acts)),
        grid=(n // tm,),
        in_specs=[_tok_spec(D_MODEL, tm), _mod_spec(tm, tokens_per_row)]
        + [_tok_spec(a.shape[1], tm) for a in acts]
        + [_resident(w_out.shape)],
        out_specs=_tok_spec(D_MODEL, tm),
        out_shape=jax.ShapeDtypeStruct((n, D_MODEL), F32),
        compiler_params=_cparams("arbitrary"),
        name="mixer_out",
    )(x2d, mrows, *acts, w_out)


def _even_in_kernel(*refs, latent):
    if latent:
        x_ref, m_ref, ng_ref, w_ref, qg_ref, kg_ref, cos_ref, sin_ref = refs[:8]
        outs = refs[8:]
    else:
        x_ref, m_ref, ng_ref, w_ref, qg_ref, kg_ref = refs[:6]
        outs = refs[6:]
    xr_ref, gt_ref, q_ref, k_ref, v_ref = outs
    h = _rms_mod(x_ref[...], ng_ref[...], m_ref[3:4, :], m_ref[4:5, :]).astype(BF16)
    xr_ref[...] = _dot(h, w_ref[:, 0:D_RNN])
    gt_ref[...] = _dot(h, w_ref[:, D_RNN:2 * D_RNN])
    base = 2 * D_RNN
    q = _dot(h, w_ref[:, base:base + DIFF_W])
    k = _dot(h, w_ref[:, base + DIFF_W:base + 2 * DIFF_W])
    v_ref[...] = _dot(h, w_ref[:, base + 2 * DIFF_W:base + 3 * DIFF_W]).astype(v_ref.dtype)
    for hh in range(DIFF_HEADS):
        cols = slice(hh * LANES, (hh + 1) * LANES)
        qh = _head_norm(q[:, cols], qg_ref[...])
        kh = _head_norm(k[:, cols], kg_ref[...])
        if latent:
            qh = _rope(qh, cos_ref[...], sin_ref[...])
            kh = _rope(kh, cos_ref[...], sin_ref[...])
        q_ref[:, cols] = (qh * (DIFF_HD ** -0.5)).astype(BF16)
        k_ref[:, cols] = kh.astype(k_ref.dtype)


def _even_in(x2d, mrows, tokens_per_row, ng, w_in, qg, kg, rope):
    n = x2d.shape[0]
    tm = TOKEN_TILE
    latent = rope is not None
    in_specs = [
        _tok_spec(D_MODEL, tm), _mod_spec(tm, tokens_per_row), _resident((1, D_MODEL)),
        _resident((D_MODEL, EVEN_IN)), _resident((1, LANES)), _resident((1, LANES)),
    ]
    args = [x2d, mrows, ng, w_in, qg, kg]
    if latent:
        seq = rope[0].shape[0]
        tab = pl.BlockSpec((tm, LANES), lambda i: (i % (seq // tm), 0))
        in_specs += [tab, tab]
        args += list(rope)
    kv_dtype = BF16 if latent else F32
    return pl.pallas_call(
        functools.partial(_even_in_kernel, latent=latent),
        grid=(n // tm,),
        in_specs=in_specs,
        out_specs=[_tok_spec(D_RNN, tm)] * 2 + [_tok_spec(DIFF_W, tm)] * 3,
        out_shape=[
            jax.ShapeDtypeStruct((n, D_RNN), F32), jax.ShapeDtypeStruct((n, D_RNN), F32),
            jax.ShapeDtypeStruct((n, DIFF_W), BF16),
            jax.ShapeDtypeStruct((n, DIFF_W), kv_dtype), jax.ShapeDtypeStruct((n, DIFF_W), kv_dtype),
        ],
        compiler_params=_cparams("arbitrary"),
        name="even_in",
    )(*args)


def _gelu_tanh(x):
    return 0.5 * x * (1.0 + jnp.tanh(math.sqrt(2.0 / math.pi) * (x + 0.044715 * (x * x * x))))


def _lru_kernel(xr_ref, gt_ref, cw_ref, cb_ref, wg_ref, bg_ref, lam_ref, h0_ref,
                y_ref, last_ref, hf_scr, hb_scr, af0, uf0, ab0, ub0, af1, uf1, ab1, ub1, *, seq, tt):
    n_chunks = seq // tt
    rows = tt * SUBLANES
    lam = lam_ref[...]
    m2sp = (-0.25 * LRU_C) * (jnp.maximum(-lam, 0.0) + jnp.log1p(jnp.exp(-jnp.abs(lam))))

    def time_major(ref, t0, n):
        return jnp.swapaxes(ref[:, pl.ds(pl.multiple_of(t0, SUBLANES), n), :], 0, 1)

    def conv_chunk(c):
        t0 = c * tt
        before = time_major(xr_ref, jnp.maximum(t0 - SUBLANES, 0), SUBLANES)[SUBLANES - 2:]
        after = time_major(xr_ref, jnp.minimum(t0 + tt, seq - SUBLANES), SUBLANES)[:1]
        before = jnp.where(c > 0, before, 0.0)
        after = jnp.where(c < n_chunks - 1, after, 0.0)
        xw = jnp.concatenate([before, time_major(xr_ref, t0, tt), after], axis=0)
        xc = cb_ref[...]
        for tap in range(4):
            xc = xc + xw[tap:tap + tt] * cw_ref[tap:tap + 1, :]
        return xc.reshape(rows, LANES)

    def decay_and_input(th_r, th_i, xc, m2sp_row):
        t = jnp.tanh(m2sp_row * th_r + m2sp_row)
        inv = 1.0 / (1.0 - t)
        a = (1.0 + t) * inv
        nt = -t
        root = jnp.where(nt > 0.0, nt * lax.rsqrt(nt), 0.0)
        u = (root * inv) * ((th_i + 1.0) * xc)
        return a.reshape(tt, SUBLANES, LANES), u.reshape(tt, SUBLANES, LANES)

    def gates(c, d):
        xc = conv_chunk(c)
        cols = slice(2 * d * LANES, 2 * (d + 1) * LANES)
        th = jnp.tanh(_dot(xc.astype(BF16), wg_ref[:, cols]) + bg_ref[:, cols])
        return decay_and_input(th[:, :LANES], th[:, LANES:], xc, m2sp[d:d + 1, :])

    bufs = ((af0, uf0, ab0, ub0), (af1, uf1, ab1, ub1))

    def fill(c, buf):
        buf[0][...], buf[1][...] = gates(c, 0)
        buf[2][...], buf[3][...] = gates(n_chunks - 1 - c, 1)

    def scan(c, buf, carry):
        af, uf, ab, ub = buf
        hf, hb = carry
        f0 = c * tt
        b0 = (n_chunks - 1 - c) * tt
        for i in range(tt):
            hf = af[i] * hf + uf[i]
            hf_scr[f0 + i] = hf
            ib = tt - 1 - i
            hb = ab[ib] * hb + ub[ib]
            hb_scr[b0 + ib] = hb
        return hf, hb

    def two_trips(k, carry):
        c = 2 * k
        fill(c + 1, bufs[1])
        carry = scan(c, bufs[0], carry)
        fill(jnp.minimum(c + 2, n_chunks - 1), bufs[0])
        return scan(c + 1, bufs[1], carry)

    fill(0, bufs[0])
    hf, hb = lax.fori_loop(0, n_chunks // 2, two_trips, (h0_ref[:, 0, :], h0_ref[:, 1, :]))
    last_ref[:, 0, :] = hf
    last_ref[:, 1, :] = hb

    def finish(c, _):
        t0 = pl.multiple_of(c * tt, tt)
        y = (hf_scr[pl.ds(t0, tt)] + hb_scr[pl.ds(t0, tt)]) * _gelu_tanh(time_major(gt_ref, t0, tt))
        y_ref[:, pl.ds(t0, tt), :] = jnp.swapaxes(y, 0, 1).astype(y_ref.dtype)
        return 0

    lax.fori_loop(0, n_chunks, finish, 0)


def _lru(xr, gt, conv_w, conv_b, wg, bg, lam, h0):
    b, seq, _ = xr.shape
    tt = min(LRU_STEPS, seq)
    blk_bytes = seq * SUBLANES * LANES * 4
    mode = dict(pipeline_mode=pl.Buffered(1)) if blk_bytes > LRU_DOUBLE_BUFFER_MAX else {}
    seq_in = pl.BlockSpec((SUBLANES, seq, LANES), lambda i, j: (i, 0, j), **mode)
    seq_out = pl.BlockSpec((SUBLANES, seq, LANES), lambda i, j: (i, 0, j))
    st_spec = pl.BlockSpec((SUBLANES, 2, LANES), lambda i, j: (i, 0, j))
    return pl.pallas_call(
        functools.partial(_lru_kernel, seq=seq, tt=tt),
        grid=(b // SUBLANES, D_RNN // LANES),
        in_specs=[seq_in, seq_in,
                  pl.BlockSpec((4, LANES), lambda i, j: (0, j)), pl.BlockSpec((1, LANES), lambda i, j: (0, j)),
                  pl.BlockSpec((None, LANES, 4 * LANES), lambda i, j: (j, 0, 0)),
                  pl.BlockSpec((None, 1, 4 * LANES), lambda i, j: (j, 0, 0)),
                  pl.BlockSpec((2, LANES), lambda i, j: (0, j)), st_spec],
        out_specs=[seq_out, st_spec],
        out_shape=[jax.ShapeDtypeStruct((b, seq, D_RNN), BF16), jax.ShapeDtypeStruct((b, 2, D_RNN), F32)],
        scratch_shapes=[pltpu.VMEM((seq, SUBLANES, LANES), F32)] * 2 + [pltpu.VMEM((tt, SUBLANES, LANES), F32)] * 8,
        compiler_params=_cparams("arbitrary", "arbitrary"),
        name="rglru",
    )(xr, gt, conv_w, conv_b, wg, bg, lam, h0)


def _diff_attn_kernel(*refs, n_src, lam_init):
    q_ref, lamv_ref, sg_ref = refs[:3]
    srcs = [(refs[3 + 2 * s], refs[4 + 2 * s]) for s in range(n_src)]
    o_ref = refs[3 + 2 * n_src]
    scr = refs[4 + 2 * n_src:]
    kb = [scr[2 * s] for s in range(n_src)]
    vt = [scr[2 * s + 1] for s in range(n_src)]

    @pl.when(pl.program_id(1) == 0)
    def _():
        for s, (k_ref, v_ref) in enumerate(srcs):
            kb[s][...] = k_ref[...].astype(BF16)
            for hh in range(DIFF_HEADS):
                cols = slice(hh * LANES, (hh + 1) * LANES)
                vt[s][hh, 0:LANES, :] = v_ref[:, cols].astype(F32).T.astype(BF16)
                vt[s][hh, LANES:, :] = jnp.ones((ONES_ROWS, v_ref.shape[0]), BF16)

    lv = lamv_ref[...]
    lam = (jnp.exp(jnp.sum(lv[0:1, :] * lv[1:2, :], axis=-1, keepdims=True))
           - jnp.exp(jnp.sum(lv[2:3, :] * lv[3:4, :], axis=-1, keepdims=True)) + lam_init)
    lo = _lo_mask()
    zero = jnp.zeros((), BF16)
    tq = q_ref.shape[0]

    def scores(hh):
        cols = slice(hh * LANES, (hh + 1) * LANES)
        qh = q_ref[:, cols]
        qs = jnp.concatenate([jnp.where(lo, qh, zero), jnp.where(lo, zero, qh)], axis=0)
        return [_dot_nt(kb[s][:, cols], qs) for s in range(n_src)]

    sts_next = scores(0)
    for hh in range(DIFF_HEADS):
        cols = slice(hh * LANES, (hh + 1) * LANES)
        sts = sts_next
        if hh + 1 < DIFF_HEADS:
            sts_next = scores(hh + 1)
        m = functools.reduce(jnp.maximum, [jnp.max(st, axis=0, keepdims=True) for st in sts])
        ps = [jnp.exp(st - m) for st in sts]
        o2t = functools.reduce(jnp.add, [_dot(vt[s][hh], ps[s].astype(BF16)) for s in range(n_src)])
        o2t = o2t[0:LANES, :] * (1.0 / o2t[LANES:LANES + 1, :])
        o = (o2t[:, :tq] - lam * o2t[:, tq:]).T
        ms = jnp.mean(o * o, axis=-1, keepdims=True)
        o = o * lax.rsqrt(ms + EPS) * sg_ref[...] * (1.0 - lam_init)
        o_ref[:, cols] = o.astype(o_ref.dtype)


def _diff_attn(q, lam_vec, subln_g, srcs, lam_init):
    b, seq, _ = q.shape
    tq = min(DIFF_Q_TILE, seq)
    q_spec = pl.BlockSpec((None, tq, DIFF_W), lambda i, j: (i, j, 0))
    in_specs = [q_spec, _resident((4, DIFF_HD)), _resident((1, LANES))]
    args = [q, lam_vec, subln_g]
    scratch = []
    for k, v in srcs:
        kv_spec = pl.BlockSpec((None, k.shape[1], DIFF_W), lambda i, j: (i, 0, 0))
        in_specs += [kv_spec, kv_spec]
        args += [k, v]
        scratch += [pltpu.VMEM((k.shape[1], DIFF_W), BF16),
                    pltpu.VMEM((DIFF_HEADS, LANES + ONES_ROWS, k.shape[1]), BF16)]
    return pl.pallas_call(
        functools.partial(_diff_attn_kernel, n_src=len(srcs), lam_init=lam_init),
        grid=(b, seq // tq),
        in_specs=in_specs,
        out_specs=q_spec,
        out_shape=jax.ShapeDtypeStruct((b, seq, DIFF_W), BF16),
        scratch_shapes=scratch,
        compiler_params=_cparams("arbitrary", "arbitrary"),
        name="diff_attn",
    )(*args)


def _odd_in_kernel(*refs, latent):
    if latent:
        x_ref, m_ref, ng_ref, w_ref, qg_ref, kg_ref, cos_ref, sin_ref = refs[:8]
        q_ref, kd_ref, v_ref = refs[8:]
    else:
        x_ref, m_ref, ng_ref, w_ref, qg_ref, kg_ref = refs[:6]
        q_ref, kd_ref, k_ref, v_ref = refs[6:]
    h = _rms_mod(x_ref[...], ng_ref[...], m_ref[3:4, :], m_ref[4:5, :]).astype(BF16)
    n_q = WIN_HEADS * WIN_HD
    n_kv = WIN_KV * WIN_HD
    q = _dot(h, w_ref[:, 0:n_q])
    for blk in range(n_q // LANES):
        cols = slice(blk * LANES, (blk + 1) * LANES)
        qh = _head_norm(q[:, cols], qg_ref[...])
        if latent:
            qh = _rope(qh, cos_ref[...], sin_ref[...])
        q_ref[:, cols] = (qh * (WIN_HD ** -0.5)).astype(BF16)
    k = _dot(h, w_ref[:, n_q:n_q + n_kv])
    v = _dot(h, w_ref[:, n_q + n_kv:n_q + 2 * n_kv])
    for blk in range(n_kv // LANES):
        cols = slice(blk * LANES, (blk + 1) * LANES)
        kh = _head_norm(k[:, cols], kg_ref[...])
        if latent:
            kh = _rope(kh, cos_ref[...], sin_ref[...])
        else:
            k_ref[:, cols] = kh
        for half in range(2):
            dst = slice((2 * blk + half) * LANES, (2 * blk + half + 1) * LANES)
            kd_ref[:, dst] = _dup_half(kh, half).astype(BF16)
    v_ref[...] = v.astype(v_ref.dtype)


def _odd_in(x2d, mrows, tokens_per_row, ng, w_in, qg, kg, rope):
    n = x2d.shape[0]
    tm = TOKEN_TILE
    latent = rope is not None
    n_q = WIN_HEADS * WIN_HD
    n_kv = WIN_KV * WIN_HD
    in_specs = [
        _tok_spec(D_MODEL, tm), _mod_spec(tm, tokens_per_row), _resident((1, D_MODEL)),
        _resident((D_MODEL, ODD_IN)), _resident((1, LANES)), _resident((1, LANES)),
    ]
    args = [x2d, mrows, ng, w_in, qg, kg]
    out_specs = [_tok_spec(n_q, tm), _tok_spec(2 * n_kv, tm)]
    out_shape = [jax.ShapeDtypeStruct((n, n_q), BF16), jax.ShapeDtypeStruct((n, 2 * n_kv), BF16)]
    if latent:
        seq = rope[0].shape[0]
        tab = pl.BlockSpec((tm, LANES), lambda i: (i % (seq // tm), 0))
        in_specs += [tab, tab]
        args += list(rope)
        out_specs += [_tok_spec(n_kv, tm)]
        out_shape += [jax.ShapeDtypeStruct((n, n_kv), BF16)]
    else:
        out_specs += [_tok_spec(n_kv, tm)] * 2
        out_shape += [jax.ShapeDtypeStruct((n, n_kv), F32)] * 2
    return pl.pallas_call(
        functools.partial(_odd_in_kernel, latent=latent),
        grid=(n // tm,),
        in_specs=in_specs,
        out_specs=out_specs,
        out_shape=out_shape,
        compiler_params=_cparams("arbitrary"),
        name="odd_in",
    )(*args)


def _win_attn_kernel(*refs, latent, seq, tq):
    if latent:
        q_ref, sink_ref, kd_ref, v_ref, ck_ref, cv_ref, o_ref, vt_scr, ckd_scr, cvt_scr = refs
    else:
        q_ref, sink_ref, kd_ref, v_ref, o_ref, vt_scr = refs
    lo = _lo_mask()
    zero = jnp.zeros((), BF16)
    n_cols = WIN_G * tq
    n_kv = WIN_KV * WIN_HD

    @pl.when(pl.program_id(1) == 0)
    def _():
        srcs = [(v_ref, vt_scr)] + ([(cv_ref, cvt_scr)] if latent else [])
        for src_ref, dst_scr in srcs:
            for blk in range(n_kv // LANES):
                vt2 = src_ref[:, blk * LANES:(blk + 1) * LANES].astype(F32).T.astype(BF16)
                for half in range(2):
                    dst_scr[2 * blk + half, 0:WIN_HD, :] = vt2[half * WIN_HD:(half + 1) * WIN_HD, :]
            for j in range(WIN_KV):
                dst_scr[j, WIN_HD:, :] = jnp.ones((ONES_ROWS, src_ref.shape[0]), BF16)
        if latent:
            for j in range(WIN_KV):
                cblk = slice((j // 2) * LANES, (j // 2 + 1) * LANES)
                ckd_scr[:, j * LANES:(j + 1) * LANES] = _dup_half(ck_ref[:, cblk], j % 2).astype(BF16)

    if latent:
        span = 3 * tq
        i = pl.program_id(1)
        start = pl.multiple_of(jnp.clip((i - 1) * tq, 0, seq - span), tq)
        kpos = start + lax.broadcasted_iota(jnp.int32, (span, 1), 0)
        qpos = i * tq + (lax.broadcasted_iota(jnp.int32, (1, n_cols), 1) & (tq - 1))
        bias = jnp.where(jnp.abs(kpos - qpos) <= WINDOW, 0.0, NEG_INF)

    def scores(j):
        kcols = slice(j * LANES, (j + 1) * LANES)
        qs = []
        for pair in range(WIN_G // 2):
            qb = q_ref[:, (2 * j + pair) * LANES:(2 * j + pair + 1) * LANES]
            qs += [jnp.where(lo, qb, zero), jnp.where(lo, zero, qb)]
        qs = jnp.concatenate(qs, axis=0)
        if latent:
            return [_dot_nt(kd_ref[pl.ds(start, span), kcols], qs) + bias, _dot_nt(ckd_scr[:, kcols], qs)]
        return [_dot_nt(kd_ref[:, kcols], qs)]

    sts_next = scores(0)
    for j in range(WIN_KV):
        sts = sts_next
        if j + 1 < WIN_KV:
            sts_next = scores(j + 1)
        sink = jnp.concatenate([jnp.full((1, tq), sink_ref[WIN_G * j + g], F32) for g in range(WIN_G)], axis=1)
        vts = [vt_scr[j, :, pl.ds(start, span)], cvt_scr[j]] if latent else [vt_scr[j]]
        m = functools.reduce(jnp.maximum, [jnp.max(st, axis=0, keepdims=True) for st in sts])
        m = jnp.maximum(m, sink)
        ps = [jnp.exp(st - m) for st in sts]
        ot = functools.reduce(jnp.add, [_dot(vt, p.astype(BF16)) for vt, p in zip(vts, ps)])
        den = ot[WIN_HD:WIN_HD + 1, :] + jnp.exp(sink - m)
        ot = ot[0:WIN_HD, :] * (1.0 / den)
        for pair in range(WIN_G // 2):
            blk = 2 * j + pair
            c0 = 2 * pair * tq
            both = jnp.concatenate([ot[:, c0:c0 + tq], ot[:, c0 + tq:c0 + 2 * tq]], axis=0)
            o_ref[:, blk * LANES:(blk + 1) * LANES] = both.T.astype(o_ref.dtype)


def _win_attn(q, sink, kd, v, ctx):
    b, seq, n_q = q.shape
    n_kv = v.shape[2]
    latent = ctx is not None
    tq = WIN_Q_TILE if latent else min(seq, 2 * WIN_Q_TILE)
    q_spec = pl.BlockSpec((None, tq, n_q), lambda i, j: (i, j, 0))
    kd_spec = pl.BlockSpec((None, seq, kd.shape[2]), lambda i, j: (i, 0, 0))
    v_spec = pl.BlockSpec((None, seq, n_kv), lambda i, j: (i, 0, 0))
    in_specs = [q_spec, pl.BlockSpec(memory_space=pltpu.SMEM), kd_spec, v_spec]
    args = [q, sink, kd, v]
    scratch = [pltpu.VMEM((WIN_KV, WIN_HD + ONES_ROWS, seq), BF16)]
    if latent:
        ck, cv = ctx
        past = ck.shape[1]
        c_spec = pl.BlockSpec((None, past, n_kv), lambda i, j: (i, 0, 0))
        in_specs += [c_spec, c_spec]
        args += [ck, cv]
        scratch += [pltpu.VMEM((past, kd.shape[2]), BF16), pltpu.VMEM((WIN_KV, WIN_HD + ONES_ROWS, past), BF16)]
    return pl.pallas_call(
        functools.partial(_win_attn_kernel, latent=latent, seq=seq, tq=tq),
        grid=(b, seq // tq),
        in_specs=in_specs,
        out_specs=q_spec,
        out_shape=jax.ShapeDtypeStruct((b, seq, n_q), BF16),
        scratch_shapes=scratch,
        compiler_params=_cparams("arbitrary", "arbitrary"),
        name="win_attn",
    )(*args)


def _rope_tables(seq):
    rows = seq // GRID_W
    row = np.repeat(np.arange(rows), GRID_W).astype(np.float32)
    col = np.tile(np.arange(GRID_W), rows).astype(np.float32)
    half = DIFF_HD // 2
    inv = (ROPE_THETA ** (-np.arange(0, half, 2, dtype=np.float32) / half)).astype(np.float32)
    ang_r = row[:, None] * inv[None, :]
    ang_c = col[:, None] * inv[None, :]
    cos = np.concatenate([np.cos(ang_r), np.cos(ang_r), np.cos(ang_c), np.cos(ang_c)], axis=1)
    sin = np.concatenate([-np.sin(ang_r), np.sin(ang_r), -np.sin(ang_c), np.sin(ang_c)], axis=1)
    cos = np.tile(cos, (1, LANES // DIFF_HD)).astype(np.float32)
    sin = np.tile(sin, (1, LANES // DIFF_HD)).astype(np.float32)
    return jnp.asarray(cos), jnp.asarray(sin)


def _block_diag(w):
    eye = jnp.eye(RNN_BLOCKS, dtype=w.dtype)
    return jnp.einsum('nkj,nm->nkmj', w, eye).reshape(D_RNN, D_RNN)


def _gate_params(wa, ba, wi, bi):
    mats = [_block_diag(wa[0]), _block_diag(wi[0]), _block_diag(wa[1]), _block_diag(wi[1])]
    vecs = [ba[0], bi[0], ba[1], bi[1]]
    w_blocks, b_blocks = [], []
    for blk in range(D_RNN // LANES):
        sl = slice(blk * LANES, (blk + 1) * LANES)
        w_blocks.append(jnp.concatenate([m[sl, sl] for m in mats], axis=1))
        b_blocks.append(jnp.concatenate([v[sl] for v in vecs]).reshape(1, -1))
    return (0.5 * jnp.stack(w_blocks)).astype(BF16), 0.5 * jnp.stack(b_blocks)


def _tile_gain(g):
    return jnp.tile(g, LANES // g.shape[0]).reshape(1, LANES)


def _diff_lambda_init(layer):
    return 0.8 - 0.6 * math.exp(-0.3 * layer)


def kernel(x_prompt, x_sample, cache_diff_k, cache_diff_v, state_lru, cache_win_k, cache_win_v, c, c_ctx,
           norm_g, w_mod, b_mod, ffn_w1, ffn_w3, ffn_w2, e_w_in, e_w_out, e_conv_w, e_conv_b,
           e_lru_wa, e_lru_ba, e_lru_wi, e_lru_bi, e_lru_lam, e_q_g, e_k_g, e_lam, e_subln_g,
           o_w_in, o_w_out, o_q_g, o_k_g, o_sink):
    batch, seq, _ = x_prompt.shape
    dec_batch, dec_seq, _ = x_sample.shape
    past = cache_diff_k.shape[2]

    cond = jnp.concatenate([c_ctx[None, :], c], axis=0)
    cond = jnp.pad(cond, ((0, COND_ROWS - cond.shape[0]), (0, 0)))
    mod = _modulation(cond, w_mod, b_mod).reshape(DEPTH, COND_ROWS, N_MOD, D_MODEL)

    w1 = ffn_w1.astype(BF16)
    w3 = ffn_w3.astype(BF16)
    w2 = ffn_w2.astype(BF16)
    rope = _rope_tables(dec_seq)

    groups = [
        dict(x=x_prompt.reshape(batch * seq, D_MODEL), b=batch, s=seq, latent=False,
             rows=slice(0, 1), per_row=batch * seq),
        dict(x=x_sample.reshape(dec_batch * dec_seq, D_MODEL), b=dec_batch, s=dec_seq, latent=True,
             rows=slice(1, 1 + dec_batch), per_row=dec_seq),
    ]
    ctx_out = {}
    finals = []
    for grp in groups:
        x = grp['x']
        nb, s, latent, per_row = grp['b'], grp['s'], grp['latent'], grp['per_row']
        for l in range(DEPTH):
            j = l // 2
            mrows = mod[l, grp['rows']]
            ng = norm_g[l].reshape(3, 1, D_MODEL)
            x = _ffn(x, mrows, per_row, ng[0], w1[l, 0], w3[l, 0], w2[l, 0], 0)
            if l % 2 == 0:
                xr, gt, q, k, v = _even_in(x, mrows, per_row, ng[1], e_w_in[j].astype(BF16),
                                           _tile_gain(e_q_g[j]), _tile_gain(e_k_g[j]), rope if latent else None)
                wg, bg = _gate_params(e_lru_wa[j], e_lru_ba[j], e_lru_wi[j], e_lru_bi[j])
                h0 = state_lru[:, j] if latent else jnp.zeros((nb, 2, D_RNN), F32)
                y_rnn, last = _lru(xr.reshape(nb, s, D_RNN), gt.reshape(nb, s, D_RNN), e_conv_w[j],
                                   e_conv_b[j].reshape(1, D_RNN), wg, bg, e_lru_lam[j], h0)
                k3 = k.reshape(nb, s, DIFF_W)
                v3 = v.reshape(nb, s, DIFF_W)
                srcs = [(k3, v3)]
                if latent:
                    srcs = [(cache_diff_k[:, j].reshape(nb, past, DIFF_W),
                             cache_diff_v[:, j].reshape(nb, past, DIFF_W))] + srcs
                else:
                    ctx_out.setdefault('diff_k', []).append(k3.reshape(nb, s, DIFF_HEADS, 2 * DIFF_HD))
                    ctx_out.setdefault('diff_v', []).append(v3.reshape(nb, s, DIFF_HEADS, 2 * DIFF_HD))
                    ctx_out.setdefault('state', []).append(last)
                o = _diff_attn(q.reshape(nb, s, DIFF_W), e_lam[j], e_subln_g[j].reshape(1, LANES), srcs,
                               _diff_lambda_init(l))
                acts = [y_rnn.reshape(nb * s, D_RNN), o.reshape(nb * s, DIFF_W)]
                x = _mixer_out(x, mrows, per_row, acts, e_w_out[j].astype(BF16))
            else:
                outs = _odd_in(x, mrows, per_row, ng[1], o_w_in[j].astype(BF16),
                               _tile_gain(o_q_g[j]), _tile_gain(o_k_g[j]), rope if latent else None)
                q, kd = outs[:2]
                v = outs[-1]
                n_kv = WIN_KV * WIN_HD
                ctx = None
                if latent:
                    ctx = (cache_win_k[:, j].reshape(nb, past, n_kv), cache_win_v[:, j].reshape(nb, past, n_kv))
                else:
                    ctx_out.setdefault('win_k', []).append(outs[2].reshape(nb, s, WIN_KV, WIN_HD))
                    ctx_out.setdefault('win_v', []).append(v.reshape(nb, s, WIN_KV, WIN_HD))
                o = _win_attn(q.reshape(nb, s, ODD_MIX), o_sink[j], kd.reshape(nb, s, 2 * n_kv),
                              v.reshape(nb, s, n_kv), ctx)
                x = _mixer_out(x, mrows, per_row, [o.reshape(nb * s, ODD_MIX)], o_w_out[j].astype(BF16))
            x = _ffn(x, mrows, per_row, ng[2], w1[l, 1], w3[l, 1], w2[l, 1], 6)
        finals.append(x.reshape(nb, s, D_MODEL))

    return (finals[0], finals[1],
            jnp.stack(ctx_out['diff_k'], axis=1), jnp.stack(ctx_out['diff_v'], axis=1),
            jnp.stack(ctx_out['state'], axis=1),
            jnp.stack(ctx_out['win_k'], axis=1), jnp.stack(ctx_out['win_v'], axis=1))
```

```python
import functools
import math

import numpy as np
import jax
import jax.numpy as jnp
from jax import lax
from jax.experimental import pallas as pl
from jax.experimental.pallas import tpu as pltpu

F32 = jnp.float32
BF16 = jnp.bfloat16

D_MODEL = 1024
DEPTH = 2
N_MOD = 9
D_FF = 2816
GRID_W = 64
ROPE_THETA = 10000.0
EPS = 1e-6
NEG_INF = -1e30

D_RNN = 512
RNN_BLOCKS = 8
RNN_BW = D_RNN // RNN_BLOCKS
LRU_C = 8.0

DIFF_HEADS = 4
DIFF_HD = 64
DIFF_W = DIFF_HEADS * 2 * DIFF_HD

WIN_HEADS = 16
WIN_KV = 4
WIN_G = WIN_HEADS // WIN_KV
WIN_HD = 64
WINDOW = 128

EVEN_IN = 2 * D_RNN + 3 * DIFF_W
ODD_IN = (WIN_HEADS + 2 * WIN_KV) * WIN_HD
ODD_MIX = WIN_HEADS * WIN_HD

LANES = 128
SUBLANES = 8
HEAD_HALF = LANES // 2
ROPE_PAIR = DIFF_HD // 4
ONES_ROWS = 16

TOKEN_TILE = 512
IN_TILE = 1024
SUB_TILE = 256
FF_CHUNK = 256
MOD_COLS = 1152
COND_ROWS = 16
LRU_STEPS = 64
LRU_DOUBLE_BUFFER_MAX = 4 * 1024 * 1024
DIFF_Q_TILE = 256
WIN_Q_TILE = 128
VMEM_LIMIT = 56 * 1024 * 1024


def _cparams(*sem):
    return pltpu.CompilerParams(dimension_semantics=sem, vmem_limit_bytes=VMEM_LIMIT)


def _resident(shape):
    return pl.BlockSpec(shape, lambda *_: (0,) * len(shape), pipeline_mode=pl.Buffered(1))


def _dot(a, b):
    return jnp.dot(a, b, preferred_element_type=F32)


def _dot_nt(a, b):
    return lax.dot_general(a, b, (((1,), (1,)), ((), ())), preferred_element_type=F32)


def _rms_mod(x, ng, shift, scale):
    ms = jnp.mean(x * x, axis=-1, keepdims=True)
    y = x * lax.rsqrt(ms + EPS) * ng
    return y * (1.0 + scale) + shift


def _silu(a):
    return a * jax.nn.sigmoid(a)


def _lo_mask():
    return lax.broadcasted_iota(jnp.int32, (1, LANES), 1) < HEAD_HALF


def _head_norm(t, g):
    lo = _lo_mask()
    sq = t * t
    s_lo = jnp.sum(jnp.where(lo, sq, 0.0), axis=-1, keepdims=True)
    s_hi = jnp.sum(jnp.where(lo, 0.0, sq), axis=-1, keepdims=True)
    inv = jnp.where(lo, lax.rsqrt(s_lo / DIFF_HD + EPS), lax.rsqrt(s_hi / DIFF_HD + EPS))
    return t * inv * g


def _head_norm_mxu(t, g, head_ones):
    sq = t * t
    hi = sq.astype(BF16)
    lo = (sq - hi.astype(F32)).astype(BF16)
    ss = _dot(hi, head_ones) + _dot(lo, head_ones)
    return t * lax.rsqrt(ss / DIFF_HD + EPS) * g


def _rope(t, cos, sin_signed):
    lane = lax.broadcasted_iota(jnp.int32, (1, LANES), 1)
    first = (lane % (2 * ROPE_PAIR)) < ROPE_PAIR
    partner = jnp.where(first, pltpu.roll(t, LANES - ROPE_PAIR, axis=1), pltpu.roll(t, ROPE_PAIR, axis=1))
    return t * cos + partner * sin_signed


def _swap_halves(t):
    return pltpu.roll(t, HEAD_HALF, axis=1)


def _dup_half(t, which):
    lo = _lo_mask()
    r = _swap_halves(t)
    return jnp.where(lo, t, r) if which == 0 else jnp.where(lo, r, t)


def _staggered(n_sub, prologue, matmuls, epilogue):
    h = prologue(0)
    pending = None
    for s in range(n_sub):
        d = matmuls(s, h)
        if s + 1 < n_sub:
            h = prologue(s + 1)
        if pending is not None:
            epilogue(*pending)
        pending = (s, d)
    epilogue(*pending)


def _mod_kernel(c_ref, w_ref, b_ref, o_ref):
    c = c_ref[...]
    s = _silu(c).astype(BF16)
    o_ref[...] = _dot(s, w_ref[...].astype(BF16)) + b_ref[...]


def _modulation(cond, w_mod, b_mod):
    n_col = N_MOD * D_MODEL
    return pl.pallas_call(
        _mod_kernel,
        grid=(DEPTH, n_col // MOD_COLS),
        in_specs=[
            pl.BlockSpec((COND_ROWS, D_MODEL), lambda l, j: (0, 0)),
            pl.BlockSpec((None, D_MODEL, MOD_COLS), lambda l, j: (l, 0, j)),
            pl.BlockSpec((None, 1, MOD_COLS), lambda l, j: (l, 0, j)),
        ],
        out_specs=pl.BlockSpec((None, COND_ROWS, MOD_COLS), lambda l, j: (l, 0, j)),
        out_shape=jax.ShapeDtypeStruct((DEPTH, COND_ROWS, n_col), F32),
        compiler_params=_cparams("arbitrary", "arbitrary"),
        name="modulation",
    )(cond, w_mod, b_mod.reshape(DEPTH, 1, n_col))


def _tok_spec(width, tm):
    return pl.BlockSpec((tm, width), lambda i: (i, 0))


def _mod_spec(tm, tokens_per_row):
    return pl.BlockSpec((None, N_MOD, D_MODEL), lambda i: ((i * tm) // tokens_per_row, 0, 0))


def _ffn_kernel(*refs, mi, n_act):
    x_ref, m_ref, ng_ref, w1_ref, w3_ref, w2_ref = refs[:6]
    acts = refs[6:6 + n_act]
    rest = refs[6 + n_act:]
    if n_act:
        wo_ref, o_ref, g_scr = rest
    else:
        o_ref, g_scr = rest
    x = x_ref[...]
    if n_act:
        y = None
        row = 0
        for a_ref in acts:
            ka = a_ref.shape[1]
            part = _dot(a_ref[...], wo_ref[row:row + ka, :])
            y = part if y is None else y + part
            row += ka
        x = x + m_ref[5:6, :] * y
    h = _rms_mod(x, ng_ref[...], m_ref[mi:mi + 1, :], m_ref[mi + 1:mi + 2, :]).astype(BF16)
    for j in range(D_FF // FF_CHUNK):
        cols = slice(j * FF_CHUNK, (j + 1) * FF_CHUNK)
        a = _dot(h, w1_ref[:, cols])
        b = _dot(h, w3_ref[:, cols])
        g_scr[:, cols] = (_silu(a) * b).astype(BF16)
    y = _dot(g_scr[...], w2_ref[...])
    o_ref[...] = x + 0.5 * m_ref[mi + 2:mi + 3, :] * y


def _ffn(x2d, mrows, tokens_per_row, ng, w1, w3, w2, mi, acts=(), w_out=None):
    n = x2d.shape[0]
    tm = TOKEN_TILE
    in_specs = [
        _tok_spec(D_MODEL, tm),
        _mod_spec(tm, tokens_per_row),
        _resident((1, D_MODEL)),
        _resident((D_MODEL, D_FF)),
        _resident((D_MODEL, D_FF)),
        _resident((D_FF, D_MODEL)),
    ] + [_tok_spec(a.shape[1], tm) for a in acts]
    args = [x2d, mrows, ng, w1, w3, w2, *acts]
    if acts:
        in_specs.append(_resident(w_out.shape))
        args.append(w_out)
    return pl.pallas_call(
        functools.partial(_ffn_kernel, mi=mi, n_act=len(acts)),
        grid=(n // tm,),
        in_specs=in_specs,
        out_specs=_tok_spec(D_MODEL, tm),
        out_shape=jax.ShapeDtypeStruct((n, D_MODEL), F32),
        scratch_shapes=[pltpu.VMEM((tm, D_FF), BF16)],
        compiler_params=_cparams("arbitrary"),
        name="swiglu",
    )(*args)


def _even_in_kernel(*refs, latent):
    if latent:
        x_ref, m_ref, ng_ref, w_ref, qg_ref, kg_ref, cos_ref, sin_ref = refs[:8]
        outs = refs[8:]
    else:
        x_ref, m_ref, ng_ref, w_ref, qg_ref, kg_ref = refs[:6]
        outs = refs[6:]
    xr_ref, gt_ref, q_ref, k_ref, v_ref = outs
    base = 2 * D_RNN

    def rows(s):
        return slice(s * SUB_TILE, (s + 1) * SUB_TILE)

    def prologue(s):
        return _rms_mod(x_ref[rows(s), :], ng_ref[...], m_ref[3:4, :], m_ref[4:5, :]).astype(BF16)

    def matmuls(s, h):
        r = rows(s)
        xr_ref[r, :] = _dot(h, w_ref[:, 0:D_RNN])
        gt_ref[r, :] = _dot(h, w_ref[:, D_RNN:2 * D_RNN])
        v_ref[r, :] = _dot(h, w_ref[:, base + 2 * DIFF_W:base + 3 * DIFF_W]).astype(v_ref.dtype)
        return _dot(h, w_ref[:, base:base + DIFF_W]), _dot(h, w_ref[:, base + DIFF_W:base + 2 * DIFF_W])

    def epilogue(s, qk):
        q, k = qk
        r = rows(s)
        for hh in range(DIFF_HEADS):
            cols = slice(hh * LANES, (hh + 1) * LANES)
            qh = _head_norm(q[:, cols], qg_ref[...])
            kh = _head_norm(k[:, cols], kg_ref[...])
            if latent:
                qh = _rope(qh, cos_ref[r, :], sin_ref[r, :])
                kh = _rope(kh, cos_ref[r, :], sin_ref[r, :])
            q_ref[r, cols] = (qh * (DIFF_HD ** -0.5)).astype(BF16)
            k_ref[r, cols] = kh.astype(k_ref.dtype)

    _staggered(x_ref.shape[0] // SUB_TILE, prologue, matmuls, epilogue)


def _even_in(x2d, mrows, tokens_per_row, ng, w_in, qg, kg, rope):
    n = x2d.shape[0]
    tm = IN_TILE
    latent = rope is not None
    in_specs = [
        _tok_spec(D_MODEL, tm), _mod_spec(tm, tokens_per_row), _resident((1, D_MODEL)),
        _resident((D_MODEL, EVEN_IN)), _resident((1, LANES)), _resident((1, LANES)),
    ]
    args = [x2d, mrows, ng, w_in, qg, kg]
    if latent:
        seq = rope[0].shape[0]
        tab = pl.BlockSpec((tm, LANES), lambda i: (i % (seq // tm), 0))
        in_specs += [tab, tab]
        args += list(rope)
    kv_dtype = BF16 if latent else F32
    return pl.pallas_call(
        functools.partial(_even_in_kernel, latent=latent),
        grid=(n // tm,),
        in_specs=in_specs,
        out_specs=[_tok_spec(D_RNN, tm)] * 2 + [_tok_spec(DIFF_W, tm)] * 3,
        out_shape=[
            jax.ShapeDtypeStruct((n, D_RNN), F32), jax.ShapeDtypeStruct((n, D_RNN), F32),
            jax.ShapeDtypeStruct((n, DIFF_W), BF16),
            jax.ShapeDtypeStruct((n, DIFF_W), kv_dtype), jax.ShapeDtypeStruct((n, DIFF_W), kv_dtype),
        ],
        compiler_params=_cparams("arbitrary"),
        name="even_in",
    )(*args)


def _gelu_tanh(x):
    return 0.5 * x * (1.0 + jnp.tanh(math.sqrt(2.0 / math.pi) * (x + 0.044715 * (x * x * x))))


def _lru_kernel(xr_ref, gt_ref, cw_ref, cb_ref, wg_ref, bg_ref, lam_ref, h0_ref,
                y_ref, last_ref, hf_scr, hb_scr, af0, uf0, ab0, ub0, af1, uf1, ab1, ub1, *, seq, tt):
    n_chunks = seq // tt
    rows = tt * SUBLANES
    lam = lam_ref[...]
    m2sp = (-0.25 * LRU_C) * (jnp.maximum(-lam, 0.0) + jnp.log1p(jnp.exp(-jnp.abs(lam))))

    def time_major(ref, t0, n):
        return jnp.swapaxes(ref[:, pl.ds(pl.multiple_of(t0, SUBLANES), n), :], 0, 1)

    def conv_chunk(c):
        t0 = c * tt
        before = time_major(xr_ref, jnp.maximum(t0 - SUBLANES, 0), SUBLANES)[SUBLANES - 2:]
        after = time_major(xr_ref, jnp.minimum(t0 + tt, seq - SUBLANES), SUBLANES)[:1]
        before = jnp.where(c > 0, before, 0.0)
        after = jnp.where(c < n_chunks - 1, after, 0.0)
        xw = jnp.concatenate([before, time_major(xr_ref, t0, tt), after], axis=0)
        xc = cb_ref[...]
        for tap in range(4):
            xc = xc + xw[tap:tap + tt] * cw_ref[tap:tap + 1, :]
        return xc.reshape(rows, LANES)

    def decay_and_input(th_r, th_i, xc, m2sp_row):
        t = jnp.tanh(m2sp_row * th_r + m2sp_row)
        inv = 1.0 / (1.0 - t)
        a = (1.0 + t) * inv
        nt = -t
        root = jnp.where(nt > 0.0, nt * lax.rsqrt(nt), 0.0)
        u = (root * inv) * ((th_i + 1.0) * xc)
        return a.reshape(tt, SUBLANES, LANES), u.reshape(tt, SUBLANES, LANES)

    def gates(c, d):
        xc = conv_chunk(c)
        cols = slice(2 * d * LANES, 2 * (d + 1) * LANES)
        th = jnp.tanh(_dot(xc.astype(BF16), wg_ref[:, cols]) + bg_ref[:, cols])
        return decay_and_input(th[:, :LANES], th[:, LANES:], xc, m2sp[d:d + 1, :])

    bufs = ((af0, uf0, ab0, ub0), (af1, uf1, ab1, ub1))

    def fill(c, buf):
        buf[0][...], buf[1][...] = gates(c, 0)
        buf[2][...], buf[3][...] = gates(n_chunks - 1 - c, 1)

    def scan(c, buf, carry):
        af, uf, ab, ub = buf
        hf, hb = carry
        f0 = c * tt
        b0 = (n_chunks - 1 - c) * tt
        for i in range(tt):
            hf = af[i] * hf + uf[i]
            hf_scr[f0 + i] = hf
            ib = tt - 1 - i
            hb = ab[ib] * hb + ub[ib]
            hb_scr[b0 + ib] = hb
        return hf, hb

    def two_trips(k, carry):
        c = 2 * k
        fill(c + 1, bufs[1])
        carry = scan(c, bufs[0], carry)
        fill(jnp.minimum(c + 2, n_chunks - 1), bufs[0])
        return scan(c + 1, bufs[1], carry)

    fill(0, bufs[0])
    hf, hb = lax.fori_loop(0, n_chunks // 2, two_trips, (h0_ref[:, 0, :], h0_ref[:, 1, :]))
    last_ref[:, 0, :] = hf
    last_ref[:, 1, :] = hb

    def finish(c, _):
        t0 = pl.multiple_of(c * tt, tt)
        y = (hf_scr[pl.ds(t0, tt)] + hb_scr[pl.ds(t0, tt)]) * _gelu_tanh(time_major(gt_ref, t0, tt))
        y_ref[:, pl.ds(t0, tt), :] = jnp.swapaxes(y, 0, 1).astype(y_ref.dtype)
        return 0

    lax.fori_loop(0, n_chunks, finish, 0)


def _lru(xr, gt, conv_w, conv_b, wg, bg, lam, h0):
    b, seq, _ = xr.shape
    tt = min(LRU_STEPS, seq)
    blk_bytes = seq * SUBLANES * LANES * 4
    mode = dict(pipeline_mode=pl.Buffered(1)) if blk_bytes > LRU_DOUBLE_BUFFER_MAX else {}
    seq_in = pl.BlockSpec((SUBLANES, seq, LANES), lambda i, j: (i, 0, j), **mode)
    seq_out = pl.BlockSpec((SUBLANES, seq, LANES), lambda i, j: (i, 0, j))
    st_spec = pl.BlockSpec((SUBLANES, 2, LANES), lambda i, j: (i, 0, j))
    return pl.pallas_call(
        functools.partial(_lru_kernel, seq=seq, tt=tt),
        grid=(b // SUBLANES, D_RNN // LANES),
        in_specs=[seq_in, seq_in,
                  pl.BlockSpec((4, LANES), lambda i, j: (0, j)), pl.BlockSpec((1, LANES), lambda i, j: (0, j)),
                  pl.BlockSpec((None, LANES, 4 * LANES), lambda i, j: (j, 0, 0)),
                  pl.BlockSpec((None, 1, 4 * LANES), lambda i, j: (j, 0, 0)),
                  pl.BlockSpec((2, LANES), lambda i, j: (0, j)), st_spec],
        out_specs=[seq_out, st_spec],
        out_shape=[jax.ShapeDtypeStruct((b, seq, D_RNN), BF16), jax.ShapeDtypeStruct((b, 2, D_RNN), F32)],
        scratch_shapes=[pltpu.VMEM((seq, SUBLANES, LANES), F32)] * 2 + [pltpu.VMEM((tt, SUBLANES, LANES), F32)] * 8,
        compiler_params=_cparams("arbitrary", "arbitrary"),
        name="rglru",
    )(xr, gt, conv_w, conv_b, wg, bg, lam, h0)


def _diff_attn_kernel(*refs, n_src, lam_init):
    q_ref, lamv_ref, sg_ref = refs[:3]
    srcs = [(refs[3 + 2 * s], refs[4 + 2 * s]) for s in range(n_src)]
    o_ref = refs[3 + 2 * n_src]
    scr = refs[4 + 2 * n_src:]
    kb = [scr[2 * s] for s in range(n_src)]
    vt = [scr[2 * s + 1] for s in range(n_src)]

    @pl.when(pl.program_id(1) == 0)
    def _():
        for s, (k_ref, v_ref) in enumerate(srcs):
            kb[s][...] = k_ref[...].astype(BF16)
            for hh in range(DIFF_HEADS):
                cols = slice(hh * LANES, (hh + 1) * LANES)
                vt[s][hh, 0:LANES, :] = v_ref[:, cols].astype(F32).T.astype(BF16)
                vt[s][hh, LANES:, :] = jnp.ones((ONES_ROWS, v_ref.shape[0]), BF16)

    lv = lamv_ref[...]
    lam = (jnp.exp(jnp.sum(lv[0:1, :] * lv[1:2, :], axis=-1, keepdims=True))
           - jnp.exp(jnp.sum(lv[2:3, :] * lv[3:4, :], axis=-1, keepdims=True)) + lam_init)
    lo = _lo_mask()
    zero = jnp.zeros((), BF16)
    tq = q_ref.shape[0]

    def scores(hh):
        cols = slice(hh * LANES, (hh + 1) * LANES)
        qh = q_ref[:, cols]
        qs = jnp.concatenate([jnp.where(lo, qh, zero), jnp.where(lo, zero, qh)], axis=0)
        return [_dot_nt(kb[s][:, cols], qs) for s in range(n_src)]

    sts_next = scores(0)
    for hh in range(DIFF_HEADS):
        cols = slice(hh * LANES, (hh + 1) * LANES)
        sts = sts_next
        if hh + 1 < DIFF_HEADS:
            sts_next = scores(hh + 1)
        m = functools.reduce(jnp.maximum, [jnp.max(st, axis=0, keepdims=True) for st in sts])
        ps = [jnp.exp(st - m) for st in sts]
        o2t = functools.reduce(jnp.add, [_dot(vt[s][hh], ps[s].astype(BF16)) for s in range(n_src)])
        o2t = o2t[0:LANES, :] * (1.0 / o2t[LANES:LANES + 1, :])
        o = (o2t[:, :tq] - lam * o2t[:, tq:]).T
        ms = jnp.mean(o * o, axis=-1, keepdims=True)
        o = o * lax.rsqrt(ms + EPS) * sg_ref[...] * (1.0 - lam_init)
        o_ref[:, cols] = o.astype(o_ref.dtype)


def _diff_attn(q, lam_vec, subln_g, srcs, lam_init):
    b, seq, _ = q.shape
    tq = min(DIFF_Q_TILE, seq)
    q_spec = pl.BlockSpec((None, tq, DIFF_W), lambda i, j: (i, j, 0))
    in_specs = [q_spec, _resident((4, DIFF_HD)), _resident((1, LANES))]
    args = [q, lam_vec, subln_g]
    scratch = []
    for k, v in srcs:
        kv_spec = pl.BlockSpec((None, k.shape[1], DIFF_W), lambda i, j: (i, 0, 0))
        in_specs += [kv_spec, kv_spec]
        args += [k, v]
        scratch += [pltpu.VMEM((k.shape[1], DIFF_W), BF16),
                    pltpu.VMEM((DIFF_HEADS, LANES + ONES_ROWS, k.shape[1]), BF16)]
    return pl.pallas_call(
        functools.partial(_diff_attn_kernel, n_src=len(srcs), lam_init=lam_init),
        grid=(b, seq // tq),
        in_specs=in_specs,
        out_specs=q_spec,
        out_shape=jax.ShapeDtypeStruct((b, seq, DIFF_W), BF16),
        scratch_shapes=scratch,
        compiler_params=_cparams("arbitrary", "arbitrary"),
        name="diff_attn",
    )(*args)


def _odd_in_kernel(*refs, latent):
    if latent:
        x_ref, m_ref, ng_ref, w_ref, qg_ref, kg_ref, ones_ref, cos_ref, sin_ref = refs[:9]
        q_ref, kd_ref, v_ref = refs[9:]
    else:
        x_ref, m_ref, ng_ref, w_ref, qg_ref, kg_ref, ones_ref = refs[:7]
        q_ref, kd_ref, k_ref, v_ref = refs[7:]
    n_q = WIN_HEADS * WIN_HD
    n_kv = WIN_KV * WIN_HD

    def rows(s):
        return slice(s * SUB_TILE, (s + 1) * SUB_TILE)

    def prologue(s):
        return _rms_mod(x_ref[rows(s), :], ng_ref[...], m_ref[3:4, :], m_ref[4:5, :]).astype(BF16)

    def matmuls(s, h):
        v_ref[rows(s), :] = _dot(h, w_ref[:, n_q + n_kv:n_q + 2 * n_kv]).astype(v_ref.dtype)
        return _dot(h, w_ref[:, 0:n_q]), _dot(h, w_ref[:, n_q:n_q + n_kv])

    def epilogue(s, qk):
        q, k = qk
        r = rows(s)
        for pair in range(n_q // (2 * LANES)):
            qn = _head_norm_mxu(q[:, 2 * pair * LANES:2 * (pair + 1) * LANES], qg_ref[...], ones_ref[...])
            for half in range(2):
                cols = slice((2 * pair + half) * LANES, (2 * pair + half + 1) * LANES)
                qh = qn[:, half * LANES:(half + 1) * LANES]
                if latent:
                    qh = _rope(qh, cos_ref[r, :], sin_ref[r, :])
                q_ref[r, cols] = (qh * (WIN_HD ** -0.5)).astype(BF16)
        kn = _head_norm_mxu(k, kg_ref[...], ones_ref[...])
        for blk in range(n_kv // LANES):
            cols = slice(blk * LANES, (blk + 1) * LANES)
            kh = kn[:, cols]
            if latent:
                kh = _rope(kh, cos_ref[r, :], sin_ref[r, :])
            else:
                k_ref[r, cols] = kh
            for half in range(2):
                dst = slice((2 * blk + half) * LANES, (2 * blk + half + 1) * LANES)
                kd_ref[r, dst] = _dup_half(kh, half).astype(BF16)

    _staggered(x_ref.shape[0] // SUB_TILE, prologue, matmuls, epilogue)


def _odd_in(x2d, mrows, tokens_per_row, ng, w_in, qg, kg, rope):
    n = x2d.shape[0]
    tm = IN_TILE
    latent = rope is not None
    n_q = WIN_HEADS * WIN_HD
    n_kv = WIN_KV * WIN_HD
    in_specs = [
        _tok_spec(D_MODEL, tm), _mod_spec(tm, tokens_per_row), _resident((1, D_MODEL)),
        _resident((D_MODEL, ODD_IN)), _resident((1, 2 * LANES)), _resident((1, 2 * LANES)),
        _resident((2 * LANES, 2 * LANES)),
    ]
    args = [x2d, mrows, ng, w_in, qg, kg, _head_ones()]
    out_specs = [_tok_spec(n_q, tm), _tok_spec(2 * n_kv, tm)]
    out_shape = [jax.ShapeDtypeStruct((n, n_q), BF16), jax.ShapeDtypeStruct((n, 2 * n_kv), BF16)]
    if latent:
        seq = rope[0].shape[0]
        tab = pl.BlockSpec((tm, LANES), lambda i: (i % (seq // tm), 0))
        in_specs += [tab, tab]
        args += list(rope)
        out_specs += [_tok_spec(n_kv, tm)]
        out_shape += [jax.ShapeDtypeStruct((n, n_kv), BF16)]
    else:
        out_specs += [_tok_spec(n_kv, tm)] * 2
        out_shape += [jax.ShapeDtypeStruct((n, n_kv), F32)] * 2
    return pl.pallas_call(
        functools.partial(_odd_in_kernel, latent=latent),
        grid=(n // tm,),
        in_specs=in_specs,
        out_specs=out_specs,
        out_shape=out_shape,
        compiler_params=_cparams("arbitrary"),
        name="odd_in",
    )(*args)


def _win_attn_kernel(*refs, latent, seq, tq):
    if latent:
        q_ref, sink_ref, kd_ref, v_ref, ck_ref, cv_ref, o_ref, vt_scr, ckd_scr, cvt_scr = refs
    else:
        q_ref, sink_ref, kd_ref, v_ref, o_ref, vt_scr = refs
    lo = _lo_mask()
    zero = jnp.zeros((), BF16)
    n_cols = WIN_G * tq
    n_kv = WIN_KV * WIN_HD

    @pl.when(pl.program_id(1) == 0)
    def _():
        srcs = [(v_ref, vt_scr)] + ([(cv_ref, cvt_scr)] if latent else [])
        for src_ref, dst_scr in srcs:
            for blk in range(n_kv // LANES):
                vt2 = src_ref[:, blk * LANES:(blk + 1) * LANES].astype(F32).T.astype(BF16)
                for half in range(2):
                    dst_scr[2 * blk + half, 0:WIN_HD, :] = vt2[half * WIN_HD:(half + 1) * WIN_HD, :]
            for j in range(WIN_KV):
                dst_scr[j, WIN_HD:, :] = jnp.ones((ONES_ROWS, src_ref.shape[0]), BF16)
        if latent:
            for j in range(WIN_KV):
                cblk = slice((j // 2) * LANES, (j // 2 + 1) * LANES)
                ckd_scr[:, j * LANES:(j + 1) * LANES] = _dup_half(ck_ref[:, cblk], j % 2).astype(BF16)

    if latent:
        span = 3 * tq
        i = pl.program_id(1)
        start = pl.multiple_of(jnp.clip((i - 1) * tq, 0, seq - span), tq)
        kpos = start + lax.broadcasted_iota(jnp.int32, (span, 1), 0)
        qpos = i * tq + (lax.broadcasted_iota(jnp.int32, (1, n_cols), 1) & (tq - 1))
        bias = jnp.where(jnp.abs(kpos - qpos) <= WINDOW, 0.0, NEG_INF)

    def scores(j):
        kcols = slice(j * LANES, (j + 1) * LANES)
        qs = []
        for pair in range(WIN_G // 2):
            qb = q_ref[:, (2 * j + pair) * LANES:(2 * j + pair + 1) * LANES]
            qs += [jnp.where(lo, qb, zero), jnp.where(lo, zero, qb)]
        qs = jnp.concatenate(qs, axis=0)
        if latent:
            return [_dot_nt(kd_ref[pl.ds(start, span), kcols], qs) + bias, _dot_nt(ckd_scr[:, kcols], qs)]
        return [_dot_nt(kd_ref[:, kcols], qs)]

    sts_next = scores(0)
    for j in range(WIN_KV):
        sts = sts_next
        if j + 1 < WIN_KV:
            sts_next = scores(j + 1)
        sink = jnp.concatenate([jnp.full((1, tq), sink_ref[WIN_G * j + g], F32) for g in range(WIN_G)], axis=1)
        vts = [vt_scr[j, :, pl.ds(start, span)], cvt_scr[j]] if latent else [vt_scr[j]]
        m = functools.reduce(jnp.maximum, [jnp.max(st, axis=0, keepdims=True) for st in sts])
        m = jnp.maximum(m, sink)
        ps = [jnp.exp(st - m) for st in sts]
        ot = functools.reduce(jnp.add, [_dot(vt, p.astype(BF16)) for vt, p in zip(vts, ps)])
        den = ot[WIN_HD:WIN_HD + 1, :] + jnp.exp(sink - m)
        ot = ot[0:WIN_HD, :] * (1.0 / den)
        for pair in range(WIN_G // 2):
            blk = 2 * j + pair
            c0 = 2 * pair * tq
            both = jnp.concatenate([ot[:, c0:c0 + tq], ot[:, c0 + tq:c0 + 2 * tq]], axis=0)
            o_ref[:, blk * LANES:(blk + 1) * LANES] = both.T.astype(o_ref.dtype)


def _win_attn(q, sink, kd, v, ctx):
    b, seq, n_q = q.shape
    n_kv = v.shape[2]
    latent = ctx is not None
    tq = WIN_Q_TILE if latent else min(seq, 2 * WIN_Q_TILE)
    q_spec = pl.BlockSpec((None, tq, n_q), lambda i, j: (i, j, 0))
    kd_spec = pl.BlockSpec((None, seq, kd.shape[2]), lambda i, j: (i, 0, 0))
    v_spec = pl.BlockSpec((None, seq, n_kv), lambda i, j: (i, 0, 0))
    in_specs = [q_spec, pl.BlockSpec(memory_space=pltpu.SMEM), kd_spec, v_spec]
    args = [q, sink, kd, v]
    scratch = [pltpu.VMEM((WIN_KV, WIN_HD + ONES_ROWS, seq), BF16)]
    if latent:
        ck, cv = ctx
        past = ck.shape[1]
        c_spec = pl.BlockSpec((None, past, n_kv), lambda i, j: (i, 0, 0))
        in_specs += [c_spec, c_spec]
        args += [ck, cv]
        scratch += [pltpu.VMEM((past, kd.shape[2]), BF16), pltpu.VMEM((WIN_KV, WIN_HD + ONES_ROWS, past), BF16)]
    return pl.pallas_call(
        functools.partial(_win_attn_kernel, latent=latent, seq=seq, tq=tq),
        grid=(b, seq // tq),
        in_specs=in_specs,
        out_specs=q_spec,
        out_shape=jax.ShapeDtypeStruct((b, seq, n_q), BF16),
        scratch_shapes=scratch,
        compiler_params=_cparams("arbitrary", "arbitrary"),
        name="win_attn",
    )(*args)


def _rope_tables(seq):
    rows = seq // GRID_W
    row = np.repeat(np.arange(rows), GRID_W).astype(np.float32)
    col = np.tile(np.arange(GRID_W), rows).astype(np.float32)
    half = DIFF_HD // 2
    inv = (ROPE_THETA ** (-np.arange(0, half, 2, dtype=np.float32) / half)).astype(np.float32)
    ang_r = row[:, None] * inv[None, :]
    ang_c = col[:, None] * inv[None, :]
    cos = np.concatenate([np.cos(ang_r), np.cos(ang_r), np.cos(ang_c), np.cos(ang_c)], axis=1)
    sin = np.concatenate([-np.sin(ang_r), np.sin(ang_r), -np.sin(ang_c), np.sin(ang_c)], axis=1)
    cos = np.tile(cos, (1, LANES // DIFF_HD)).astype(np.float32)
    sin = np.tile(sin, (1, LANES // DIFF_HD)).astype(np.float32)
    return jnp.asarray(cos), jnp.asarray(sin)


def _block_diag(w):
    eye = jnp.eye(RNN_BLOCKS, dtype=w.dtype)
    return jnp.einsum('nkj,nm->nkmj', w, eye).reshape(D_RNN, D_RNN)


def _gate_params(wa, ba, wi, bi):
    mats = [_block_diag(wa[0]), _block_diag(wi[0]), _block_diag(wa[1]), _block_diag(wi[1])]
    vecs = [ba[0], bi[0], ba[1], bi[1]]
    w_blocks, b_blocks = [], []
    for blk in range(D_RNN // LANES):
        sl = slice(blk * LANES, (blk + 1) * LANES)
        w_blocks.append(jnp.concatenate([m[sl, sl] for m in mats], axis=1))
        b_blocks.append(jnp.concatenate([v[sl] for v in vecs]).reshape(1, -1))
    return (0.5 * jnp.stack(w_blocks)).astype(BF16), 0.5 * jnp.stack(b_blocks)


def _tile_gain(g, width=LANES):
    return jnp.tile(g, width // g.shape[0]).reshape(1, width)


def _head_ones():
    head = np.arange(2 * LANES) // WIN_HD
    return jnp.asarray((head[:, None] == head[None, :]).astype(np.float32), dtype=BF16)


def _diff_lambda_init(layer):
    return 0.8 - 0.6 * math.exp(-0.3 * layer)


def kernel(x_prompt, x_sample, cache_diff_k, cache_diff_v, state_lru, cache_win_k, cache_win_v, c, c_ctx,
           norm_g, w_mod, b_mod, ffn_w1, ffn_w3, ffn_w2, e_w_in, e_w_out, e_conv_w, e_conv_b,
           e_lru_wa, e_lru_ba, e_lru_wi, e_lru_bi, e_lru_lam, e_q_g, e_k_g, e_lam, e_subln_g,
           o_w_in, o_w_out, o_q_g, o_k_g, o_sink):
    batch, seq, _ = x_prompt.shape
    dec_batch, dec_seq, _ = x_sample.shape
    past = cache_diff_k.shape[2]

    cond = jnp.concatenate([c_ctx[None, :], c], axis=0)
    cond = jnp.pad(cond, ((0, COND_ROWS - cond.shape[0]), (0, 0)))
    mod = _modulation(cond, w_mod, b_mod).reshape(DEPTH, COND_ROWS, N_MOD, D_MODEL)

    w1 = ffn_w1.astype(BF16)
    w3 = ffn_w3.astype(BF16)
    w2 = ffn_w2.astype(BF16)
    rope = _rope_tables(dec_seq)

    groups = [
        dict(x=x_prompt.reshape(batch * seq, D_MODEL), b=batch, s=seq, latent=False,
             rows=slice(0, 1), per_row=batch * seq),
        dict(x=x_sample.reshape(dec_batch * dec_seq, D_MODEL), b=dec_batch, s=dec_seq, latent=True,
             rows=slice(1, 1 + dec_batch), per_row=dec_seq),
    ]
    ctx_out = {}
    finals = []
    for grp in groups:
        x = grp['x']
        nb, s, latent, per_row = grp['b'], grp['s'], grp['latent'], grp['per_row']
        for l in range(DEPTH):
            j = l // 2
            mrows = mod[l, grp['rows']]
            ng = norm_g[l].reshape(3, 1, D_MODEL)
            x = _ffn(x, mrows, per_row, ng[0], w1[l, 0], w3[l, 0], w2[l, 0], 0)
            if l % 2 == 0:
                xr, gt, q, k, v = _even_in(x, mrows, per_row, ng[1], e_w_in[j].astype(BF16),
                                           _tile_gain(e_q_g[j]), _tile_gain(e_k_g[j]), rope if latent else None)
                wg, bg = _gate_params(e_lru_wa[j], e_lru_ba[j], e_lru_wi[j], e_lru_bi[j])
                h0 = state_lru[:, j] if latent else jnp.zeros((nb, 2, D_RNN), F32)
                y_rnn, last = _lru(xr.reshape(nb, s, D_RNN), gt.reshape(nb, s, D_RNN), e_conv_w[j],
                                   e_conv_b[j].reshape(1, D_RNN), wg, bg, e_lru_lam[j], h0)
                k3 = k.reshape(nb, s, DIFF_W)
                v3 = v.reshape(nb, s, DIFF_W)
                srcs = [(k3, v3)]
                if latent:
                    srcs = [(cache_diff_k[:, j].reshape(nb, past, DIFF_W),
                             cache_diff_v[:, j].reshape(nb, past, DIFF_W))] + srcs
                else:
                    ctx_out.setdefault('diff_k', []).append(k3.reshape(nb, s, DIFF_HEADS, 2 * DIFF_HD))
                    ctx_out.setdefault('diff_v', []).append(v3.reshape(nb, s, DIFF_HEADS, 2 * DIFF_HD))
                    ctx_out.setdefault('state', []).append(last)
                o = _diff_attn(q.reshape(nb, s, DIFF_W), e_lam[j], e_subln_g[j].reshape(1, LANES), srcs,
                               _diff_lambda_init(l))
                acts = [y_rnn.reshape(nb * s, D_RNN), o.reshape(nb * s, DIFF_W)]
                w_out = e_w_out[j].astype(BF16)
            else:
                outs = _odd_in(x, mrows, per_row, ng[1], o_w_in[j].astype(BF16),
                               _tile_gain(o_q_g[j], 2 * LANES), _tile_gain(o_k_g[j], 2 * LANES),
                               rope if latent else None)
                q, kd = outs[:2]
                v = outs[-1]
                n_kv = WIN_KV * WIN_HD
                ctx = None
                if latent:
                    ctx = (cache_win_k[:, j].reshape(nb, past, n_kv), cache_win_v[:, j].reshape(nb, past, n_kv))
                else:
                    ctx_out.setdefault('win_k', []).append(outs[2].reshape(nb, s, WIN_KV, WIN_HD))
                    ctx_out.setdefault('win_v', []).append(v.reshape(nb, s, WIN_KV, WIN_HD))
                o = _win_attn(q.reshape(nb, s, ODD_MIX), o_sink[j], kd.reshape(nb, s, 2 * n_kv),
                              v.reshape(nb, s, n_kv), ctx)
                acts = [o.reshape(nb * s, ODD_MIX)]
                w_out = o_w_out[j].astype(BF16)
            x = _ffn(x, mrows, per_row, ng[2], w1[l, 1], w3[l, 1], w2[l, 1], 6, acts, w_out)
        finals.append(x.reshape(nb, s, D_MODEL))

    return (finals[0], finals[1],
            jnp.stack(ctx_out['diff_k'], axis=1), jnp.stack(ctx_out['diff_v'], axis=1),
            jnp.stack(ctx_out['state'], axis=1),
            jnp.stack(ctx_out['win_k'], axis=1), jnp.stack(ctx_out['win_v'], axis=1))
```

```python
import functools
import math

import numpy as np
import jax
import jax.numpy as jnp
from jax import lax
from jax.experimental import pallas as pl
from jax.experimental.pallas import tpu as pltpu

F32 = jnp.float32
BF16 = jnp.bfloat16

D_MODEL = 1024
DEPTH = 2
N_MOD = 9
D_FF = 2816
GRID_W = 64
ROPE_THETA = 10000.0
EPS = 1e-6
NEG_INF = -1e30

D_RNN = 512
RNN_BLOCKS = 8
RNN_BW = D_RNN // RNN_BLOCKS
LRU_C = 8.0

DIFF_HEADS = 4
DIFF_HD = 64
DIFF_W = DIFF_HEADS * 2 * DIFF_HD

WIN_HEADS = 16
WIN_KV = 4
WIN_G = WIN_HEADS // WIN_KV
WIN_HD = 64
WINDOW = 128

EVEN_IN = 2 * D_RNN + 3 * DIFF_W
ODD_IN = (WIN_HEADS + 2 * WIN_KV) * WIN_HD
ODD_MIX = WIN_HEADS * WIN_HD

LANES = 128
SUBLANES = 8
HEAD_HALF = LANES // 2
ROPE_PAIR = DIFF_HD // 4
ONES_ROWS = 16
LOG2E = math.log2(math.e)
Q_SCALE = DIFF_HD ** -0.5 * LOG2E

TOKEN_TILE = 512
FFN_TILE = 1024
IN_TILE = 1024
SUB_TILE = 256
FF_CHUNK = 256
MOD_COLS = 1152
COND_ROWS = 16
LRU_STEPS = 64
LRU_DOUBLE_BUFFER_MAX = 4 * 1024 * 1024
DIFF_Q_TILE = 256
DIFF_KEY_CHUNK = 512
DIFF_LOOKAHEAD = 2
WIN_Q_TILE = 128
VMEM_LIMIT = 56 * 1024 * 1024


def _cparams(*sem):
    return pltpu.CompilerParams(dimension_semantics=sem, vmem_limit_bytes=VMEM_LIMIT)


def _resident(shape):
    return pl.BlockSpec(shape, lambda *_: (0,) * len(shape), pipeline_mode=pl.Buffered(1))


def _dot(a, b):
    return jnp.dot(a, b, preferred_element_type=F32)


def _dot_nt(a, b):
    return lax.dot_general(a, b, (((1,), (1,)), ((), ())), preferred_element_type=F32)


def _rms_mod(x, ng, shift, scale):
    ms = jnp.mean(x * x, axis=-1, keepdims=True)
    y = x * lax.rsqrt(ms + EPS) * ng
    return y * (1.0 + scale) + shift


def _silu(a):
    return a * jax.nn.sigmoid(a)


def _lo_mask():
    return lax.broadcasted_iota(jnp.int32, (1, LANES), 1) < HEAD_HALF


def _head_norm(t, g):
    lo = _lo_mask()
    sq = t * t
    s_lo = jnp.sum(jnp.where(lo, sq, 0.0), axis=-1, keepdims=True)
    s_hi = jnp.sum(jnp.where(lo, 0.0, sq), axis=-1, keepdims=True)
    inv = jnp.where(lo, lax.rsqrt(s_lo / DIFF_HD + EPS), lax.rsqrt(s_hi / DIFF_HD + EPS))
    return t * inv * g


def _head_norm_mxu(t, g, head_ones):
    sq = t * t
    hi = sq.astype(BF16)
    lo = (sq - hi.astype(F32)).astype(BF16)
    ss = _dot(hi, head_ones) + _dot(lo, head_ones)
    return t * lax.rsqrt(ss / DIFF_HD + EPS) * g


def _rope(t, cos, sin_signed):
    lane = lax.broadcasted_iota(jnp.int32, (1, LANES), 1)
    first = (lane % (2 * ROPE_PAIR)) < ROPE_PAIR
    partner = jnp.where(first, pltpu.roll(t, LANES - ROPE_PAIR, axis=1), pltpu.roll(t, ROPE_PAIR, axis=1))
    return t * cos + partner * sin_signed


def _swap_halves(t):
    return pltpu.roll(t, HEAD_HALF, axis=1)


def _dup_half(t, which):
    lo = _lo_mask()
    r = _swap_halves(t)
    return jnp.where(lo, t, r) if which == 0 else jnp.where(lo, r, t)


def _staggered(n_sub, prologue, matmuls, epilogue):
    h = prologue(0)
    pending = None
    for s in range(n_sub):
        d = matmuls(s, h)
        if s + 1 < n_sub:
            h = prologue(s + 1)
        if pending is not None:
            epilogue(*pending)
        pending = (s, d)
    epilogue(*pending)


def _mod_kernel(c_ref, w_ref, b_ref, o_ref):
    c = c_ref[...]
    s = _silu(c).astype(BF16)
    o_ref[...] = _dot(s, w_ref[...].astype(BF16)) + b_ref[...]


def _modulation(cond, w_mod, b_mod):
    n_col = N_MOD * D_MODEL
    return pl.pallas_call(
        _mod_kernel,
        grid=(DEPTH, n_col // MOD_COLS),
        in_specs=[
            pl.BlockSpec((COND_ROWS, D_MODEL), lambda l, j: (0, 0)),
            pl.BlockSpec((None, D_MODEL, MOD_COLS), lambda l, j: (l, 0, j)),
            pl.BlockSpec((None, 1, MOD_COLS), lambda l, j: (l, 0, j)),
        ],
        out_specs=pl.BlockSpec((None, COND_ROWS, MOD_COLS), lambda l, j: (l, 0, j)),
        out_shape=jax.ShapeDtypeStruct((DEPTH, COND_ROWS, n_col), F32),
        compiler_params=_cparams("arbitrary", "arbitrary"),
        name="modulation",
    )(cond, w_mod, b_mod.reshape(DEPTH, 1, n_col))


def _tok_spec(width, tm):
    return pl.BlockSpec((tm, width), lambda i: (i, 0))


def _mod_spec(tm, tokens_per_row):
    return pl.BlockSpec((None, N_MOD, D_MODEL), lambda i: ((i * tm) // tokens_per_row, 0, 0))


def _ffn_kernel(*refs, mi, n_act):
    x_ref, m_ref, ng_ref, w1_ref, w3_ref, w2_ref = refs[:6]
    acts = refs[6:6 + n_act]
    rest = refs[6 + n_act:]
    if n_act:
        wo_ref, o_ref, x_scr, g_scr = rest
    else:
        o_ref, x_scr, g_scr = rest

    def rows(s):
        return slice(s * TOKEN_TILE, (s + 1) * TOKEN_TILE)

    def prologue(s):
        x = x_ref[rows(s), :]
        if n_act:
            return x
        x_scr[s] = x
        return _rms_mod(x, ng_ref[...], m_ref[mi:mi + 1, :], m_ref[mi + 1:mi + 2, :]).astype(BF16)

    def matmuls(s, h):
        if n_act:
            y = None
            row = 0
            for a_ref in acts:
                ka = a_ref.shape[1]
                part = _dot(a_ref[rows(s), :], wo_ref[row:row + ka, :])
                y = part if y is None else y + part
                row += ka
            x = h + m_ref[5:6, :] * y
            x_scr[s] = x
            h = _rms_mod(x, ng_ref[...], m_ref[mi:mi + 1, :], m_ref[mi + 1:mi + 2, :]).astype(BF16)
        for j in range(D_FF // FF_CHUNK):
            cols = slice(j * FF_CHUNK, (j + 1) * FF_CHUNK)
            a = _dot(h, w1_ref[:, cols])
            b = _dot(h, w3_ref[:, cols])
            g_scr[:, cols] = (_silu(a) * b).astype(BF16)
        return _dot(g_scr[...], w2_ref[...])

    def epilogue(s, y):
        o_ref[rows(s), :] = x_scr[s] + 0.5 * m_ref[mi + 2:mi + 3, :] * y

    _staggered(x_ref.shape[0] // TOKEN_TILE, prologue, matmuls, epilogue)


def _ffn(x2d, mrows, tokens_per_row, ng, w1, w3, w2, layer, which, acts=(), w_out=None):
    n = x2d.shape[0]
    tm = FFN_TILE
    n_sub = tm // TOKEN_TILE
    mi = 6 * which

    def stack_spec(rows_, cols_):
        return pl.BlockSpec((None, None, rows_, cols_), lambda i: (layer, which, 0, 0), pipeline_mode=pl.Buffered(1))

    in_specs = [
        _tok_spec(D_MODEL, tm),
        _mod_spec(tm, tokens_per_row),
        _resident((1, D_MODEL)),
        stack_spec(D_MODEL, D_FF),
        stack_spec(D_MODEL, D_FF),
        stack_spec(D_FF, D_MODEL),
    ] + [_tok_spec(a.shape[1], tm) for a in acts]
    args = [x2d, mrows, ng, w1, w3, w2, *acts]
    if acts:
        in_specs.append(_resident(w_out.shape))
        args.append(w_out)
    return pl.pallas_call(
        functools.partial(_ffn_kernel, mi=mi, n_act=len(acts)),
        grid=(n // tm,),
        in_specs=in_specs,
        out_specs=_tok_spec(D_MODEL, tm),
        out_shape=jax.ShapeDtypeStruct((n, D_MODEL), F32),
        scratch_shapes=[pltpu.VMEM((n_sub, TOKEN_TILE, D_MODEL), F32), pltpu.VMEM((TOKEN_TILE, D_FF), BF16)],
        compiler_params=_cparams("arbitrary"),
        name="swiglu",
    )(*args)


def _even_in_kernel(*refs, latent):
    if latent:
        x_ref, m_ref, ng_ref, w_ref, qg_ref, kg_ref, cos_ref, sin_ref = refs[:8]
        outs = refs[8:]
    else:
        x_ref, m_ref, ng_ref, w_ref, qg_ref, kg_ref = refs[:6]
        outs = refs[6:]
    xr_ref, gt_ref, q_ref, k_ref, v_ref = outs
    base = 2 * D_RNN

    def rows(s):
        return slice(s * SUB_TILE, (s + 1) * SUB_TILE)

    def prologue(s):
        return _rms_mod(x_ref[rows(s), :], ng_ref[...], m_ref[3:4, :], m_ref[4:5, :]).astype(BF16)

    def matmuls(s, h):
        r = rows(s)
        xr_ref[r, :] = _dot(h, w_ref[:, 0:D_RNN])
        gt_ref[r, :] = _dot(h, w_ref[:, D_RNN:2 * D_RNN])
        v_ref[r, :] = _dot(h, w_ref[:, base + 2 * DIFF_W:base + 3 * DIFF_W]).astype(v_ref.dtype)
        return _dot(h, w_ref[:, base:base + DIFF_W]), _dot(h, w_ref[:, base + DIFF_W:base + 2 * DIFF_W])

    def epilogue(s, qk):
        q, k = qk
        r = rows(s)
        for hh in range(DIFF_HEADS):
            cols = slice(hh * LANES, (hh + 1) * LANES)
            qh = _head_norm(q[:, cols], qg_ref[...])
            kh = _head_norm(k[:, cols], kg_ref[...])
            if latent:
                qh = _rope(qh, cos_ref[r, :], sin_ref[r, :])
                kh = _rope(kh, cos_ref[r, :], sin_ref[r, :])
            q_ref[r, cols] = (qh * Q_SCALE).astype(BF16)
            k_ref[r, cols] = kh.astype(k_ref.dtype)

    _staggered(x_ref.shape[0] // SUB_TILE, prologue, matmuls, epilogue)


def _even_in(x2d, mrows, tokens_per_row, ng, w_in, qg, kg, rope):
    n = x2d.shape[0]
    tm = IN_TILE
    latent = rope is not None
    in_specs = [
        _tok_spec(D_MODEL, tm), _mod_spec(tm, tokens_per_row), _resident((1, D_MODEL)),
        _resident((D_MODEL, EVEN_IN)), _resident((1, LANES)), _resident((1, LANES)),
    ]
    args = [x2d, mrows, ng, w_in, qg, kg]
    if latent:
        seq = rope[0].shape[0]
        tab = pl.BlockSpec((tm, LANES), lambda i: (i % (seq // tm), 0))
        in_specs += [tab, tab]
        args += list(rope)
    kv_dtype = BF16 if latent else F32
    return pl.pallas_call(
        functools.partial(_even_in_kernel, latent=latent),
        grid=(n // tm,),
        in_specs=in_specs,
        out_specs=[_tok_spec(D_RNN, tm)] * 2 + [_tok_spec(DIFF_W, tm)] * 3,
        out_shape=[
            jax.ShapeDtypeStruct((n, D_RNN), F32), jax.ShapeDtypeStruct((n, D_RNN), F32),
            jax.ShapeDtypeStruct((n, DIFF_W), BF16),
            jax.ShapeDtypeStruct((n, DIFF_W), kv_dtype), jax.ShapeDtypeStruct((n, DIFF_W), kv_dtype),
        ],
        compiler_params=_cparams("arbitrary"),
        name="even_in",
    )(*args)


def _gelu_tanh(x):
    return 0.5 * x * (1.0 + jnp.tanh(math.sqrt(2.0 / math.pi) * (x + 0.044715 * (x * x * x))))


def _lru_kernel(xr_ref, gt_ref, cw_ref, cb_ref, wg_ref, bg_ref, lam_ref, h0_ref,
                y_ref, last_ref, hf_scr, hb_scr, af0, uf0, ab0, ub0, af1, uf1, ab1, ub1, *, seq, tt):
    n_chunks = seq // tt
    rows = tt * SUBLANES
    lam = lam_ref[...]
    m2sp = (-0.25 * LRU_C) * (jnp.maximum(-lam, 0.0) + jnp.log1p(jnp.exp(-jnp.abs(lam))))

    def time_major(ref, t0, n):
        return jnp.swapaxes(ref[:, pl.ds(pl.multiple_of(t0, SUBLANES), n), :], 0, 1)

    def conv_chunk(c):
        t0 = c * tt
        before = time_major(xr_ref, jnp.maximum(t0 - SUBLANES, 0), SUBLANES)[SUBLANES - 2:]
        after = time_major(xr_ref, jnp.minimum(t0 + tt, seq - SUBLANES), SUBLANES)[:1]
        before = jnp.where(c > 0, before, 0.0)
        after = jnp.where(c < n_chunks - 1, after, 0.0)
        xw = jnp.concatenate([before, time_major(xr_ref, t0, tt), after], axis=0)
        xc = cb_ref[...]
        for tap in range(4):
            xc = xc + xw[tap:tap + tt] * cw_ref[tap:tap + 1, :]
        return xc.reshape(rows, LANES)

    def decay_and_input(th_r, th_i, xc, m2sp_row):
        t = jnp.tanh(m2sp_row * th_r + m2sp_row)
        inv = 1.0 / (1.0 - t)
        a = (1.0 + t) * inv
        nt = -t
        root = jnp.where(nt > 0.0, nt * lax.rsqrt(nt), 0.0)
        u = (root * inv) * ((th_i + 1.0) * xc)
        return a.reshape(tt, SUBLANES, LANES), u.reshape(tt, SUBLANES, LANES)

    def gates(c, d):
        xc = conv_chunk(c)
        cols = slice(2 * d * LANES, 2 * (d + 1) * LANES)
        th = jnp.tanh(_dot(xc.astype(BF16), wg_ref[:, cols]) + bg_ref[:, cols])
        return decay_and_input(th[:, :LANES], th[:, LANES:], xc, m2sp[d:d + 1, :])

    bufs = ((af0, uf0, ab0, ub0), (af1, uf1, ab1, ub1))

    def fill(c, buf):
        buf[0][...], buf[1][...] = gates(c, 0)
        buf[2][...], buf[3][...] = gates(n_chunks - 1 - c, 1)

    def scan(c, buf, carry):
        af, uf, ab, ub = buf
        hf, hb = carry
        f0 = c * tt
        b0 = (n_chunks - 1 - c) * tt
        for i in range(tt):
            hf = af[i] * hf + uf[i]
            hf_scr[f0 + i] = hf
            ib = tt - 1 - i
            hb = ab[ib] * hb + ub[ib]
            hb_scr[b0 + ib] = hb
        return hf, hb

    def two_trips(k, carry):
        c = 2 * k
        fill(c + 1, bufs[1])
        carry = scan(c, bufs[0], carry)
        fill(jnp.minimum(c + 2, n_chunks - 1), bufs[0])
        return scan(c + 1, bufs[1], carry)

    fill(0, bufs[0])
    hf, hb = lax.fori_loop(0, n_chunks // 2, two_trips, (h0_ref[:, 0, :], h0_ref[:, 1, :]))
    last_ref[:, 0, :] = hf
    last_ref[:, 1, :] = hb

    def finish(c, _):
        t0 = pl.multiple_of(c * tt, tt)
        y = (hf_scr[pl.ds(t0, tt)] + hb_scr[pl.ds(t0, tt)]) * _gelu_tanh(time_major(gt_ref, t0, tt))
        y_ref[:, pl.ds(t0, tt), :] = jnp.swapaxes(y, 0, 1).astype(y_ref.dtype)
        return 0

    lax.fori_loop(0, n_chunks, finish, 0)


def _lru(xr, gt, conv_w, conv_b, wg, bg, lam, h0):
    b, seq, _ = xr.shape
    tt = min(LRU_STEPS, seq)
    blk_bytes = seq * SUBLANES * LANES * 4
    mode = dict(pipeline_mode=pl.Buffered(1)) if blk_bytes > LRU_DOUBLE_BUFFER_MAX else {}
    seq_in = pl.BlockSpec((SUBLANES, seq, LANES), lambda i, j: (i, 0, j), **mode)
    seq_out = pl.BlockSpec((SUBLANES, seq, LANES), lambda i, j: (i, 0, j))
    st_spec = pl.BlockSpec((SUBLANES, 2, LANES), lambda i, j: (i, 0, j))
    return pl.pallas_call(
        functools.partial(_lru_kernel, seq=seq, tt=tt),
        grid=(b // SUBLANES, D_RNN // LANES),
        in_specs=[seq_in, seq_in,
                  pl.BlockSpec((4, LANES), lambda i, j: (0, j)), pl.BlockSpec((1, LANES), lambda i, j: (0, j)),
                  pl.BlockSpec((None, LANES, 4 * LANES), lambda i, j: (j, 0, 0)),
                  pl.BlockSpec((None, 1, 4 * LANES), lambda i, j: (j, 0, 0)),
                  pl.BlockSpec((2, LANES), lambda i, j: (0, j)), st_spec],
        out_specs=[seq_out, st_spec],
        out_shape=[jax.ShapeDtypeStruct((b, seq, D_RNN), BF16), jax.ShapeDtypeStruct((b, 2, D_RNN), F32)],
        scratch_shapes=[pltpu.VMEM((seq, SUBLANES, LANES), F32)] * 2 + [pltpu.VMEM((tt, SUBLANES, LANES), F32)] * 8,
        compiler_params=_cparams("arbitrary", "arbitrary"),
        name="rglru",
    )(xr, gt, conv_w, conv_b, wg, bg, lam, h0)


def _diff_attn_kernel(*refs, n_src, lam_init):
    q_ref, lamv_ref, sg_ref = refs[:3]
    srcs = [(refs[3 + 2 * s], refs[4 + 2 * s]) for s in range(n_src)]
    o_ref = refs[3 + 2 * n_src]
    scr = refs[4 + 2 * n_src:]
    kb = [scr[2 * s] for s in range(n_src)]
    vt = [scr[2 * s + 1] for s in range(n_src)]

    @pl.when(pl.program_id(1) == 0)
    def _():
        for s, (k_ref, v_ref) in enumerate(srcs):
            kb[s][...] = k_ref[...].astype(BF16)
            for hh in range(DIFF_HEADS):
                cols = slice(hh * LANES, (hh + 1) * LANES)
                vt[s][hh, 0:LANES, :] = v_ref[:, cols].astype(F32).T.astype(BF16)
                vt[s][hh, LANES:, :] = jnp.ones((ONES_ROWS, v_ref.shape[0]), BF16)

    lv = lamv_ref[...]
    lam = (jnp.exp(jnp.sum(lv[0:1, :] * lv[1:2, :], axis=-1, keepdims=True))
           - jnp.exp(jnp.sum(lv[2:3, :] * lv[3:4, :], axis=-1, keepdims=True)) + lam_init)
    lo = _lo_mask()
    zero = jnp.zeros((), BF16)
    tq = q_ref.shape[0]

    chunks = []
    for s, (k_ref, _) in enumerate(srcs):
        t_s = k_ref.shape[0]
        step = min(DIFF_KEY_CHUNK, t_s)
        chunks += [(s, t0, step) for t0 in range(0, t_s, step)]
    stages = [(hh, ci) for hh in range(DIFF_HEADS) for ci in range(len(chunks))]

    def masked_queries(hh):
        qh = q_ref[:, hh * LANES:(hh + 1) * LANES]
        return jnp.concatenate([jnp.where(lo, qh, zero), jnp.where(lo, zero, qh)], axis=0)

    qs = [masked_queries(hh) for hh in range(DIFF_HEADS)]

    def scores(hh, ci):
        s, t0, n = chunks[ci]
        return _dot_nt(kb[s][t0:t0 + n, hh * LANES:(hh + 1) * LANES], qs[hh])

    queue = [scores(*stages[i]) for i in range(min(DIFF_LOOKAHEAD, len(stages)))]
    m = acc = None
    for idx, (hh, ci) in enumerate(stages):
        st = queue.pop(0)
        if idx + DIFF_LOOKAHEAD < len(stages):
            queue.append(scores(*stages[idx + DIFF_LOOKAHEAD]))
        s, t0, n = chunks[ci]
        vt_c = vt[s][hh, :, t0:t0 + n]
        m_c = jnp.max(st, axis=0, keepdims=True)
        if ci == 0:
            m = m_c
            acc = _dot(vt_c, jnp.exp2(st - m).astype(BF16))
        else:
            m_new = jnp.maximum(m, m_c)
            acc = acc * jnp.exp2(m - m_new) + _dot(vt_c, jnp.exp2(st - m_new).astype(BF16))
            m = m_new
        if ci == len(chunks) - 1:
            cols = slice(hh * LANES, (hh + 1) * LANES)
            o2t = acc[0:LANES, :] * (1.0 / acc[LANES:LANES + 1, :])
            o = (o2t[:, :tq] - lam * o2t[:, tq:]).T
            ms = jnp.mean(o * o, axis=-1, keepdims=True)
            o = o * lax.rsqrt(ms + EPS) * sg_ref[...] * (1.0 - lam_init)
            o_ref[:, cols] = o.astype(o_ref.dtype)


def _diff_attn(q, lam_vec, subln_g, srcs, lam_init):
    b, seq, _ = q.shape
    tq = min(DIFF_Q_TILE, seq)
    q_spec = pl.BlockSpec((None, tq, DIFF_W), lambda i, j: (i, j, 0))
    in_specs = [q_spec, _resident((4, DIFF_HD)), _resident((1, LANES))]
    args = [q, lam_vec, subln_g]
    scratch = []
    for k, v in srcs:
        kv_spec = pl.BlockSpec((None, k.shape[1], DIFF_W), lambda i, j: (i, 0, 0))
        in_specs += [kv_spec, kv_spec]
        args += [k, v]
        scratch += [pltpu.VMEM((k.shape[1], DIFF_W), BF16),
                    pltpu.VMEM((DIFF_HEADS, LANES + ONES_ROWS, k.shape[1]), BF16)]
    return pl.pallas_call(
        functools.partial(_diff_attn_kernel, n_src=len(srcs), lam_init=lam_init),
        grid=(b, seq // tq),
        in_specs=in_specs,
        out_specs=q_spec,
        out_shape=jax.ShapeDtypeStruct((b, seq, DIFF_W), BF16),
        scratch_shapes=scratch,
        compiler_params=_cparams("arbitrary", "arbitrary"),
        name="diff_attn",
    )(*args)


def _odd_in_kernel(*refs, latent):
    if latent:
        x_ref, m_ref, ng_ref, w_ref, qg_ref, kg_ref, ones_ref, cos_ref, sin_ref = refs[:9]
        q_ref, kd_ref, v_ref = refs[9:]
    else:
        x_ref, m_ref, ng_ref, w_ref, qg_ref, kg_ref, ones_ref = refs[:7]
        q_ref, kd_ref, k_ref, v_ref = refs[7:]
    n_q = WIN_HEADS * WIN_HD
    n_kv = WIN_KV * WIN_HD

    def rows(s):
        return slice(s * SUB_TILE, (s + 1) * SUB_TILE)

    def prologue(s):
        return _rms_mod(x_ref[rows(s), :], ng_ref[...], m_ref[3:4, :], m_ref[4:5, :]).astype(BF16)

    def matmuls(s, h):
        v_ref[rows(s), :] = _dot(h, w_ref[:, n_q + n_kv:n_q + 2 * n_kv]).astype(v_ref.dtype)
        return _dot(h, w_ref[:, 0:n_q]), _dot(h, w_ref[:, n_q:n_q + n_kv])

    def epilogue(s, qk):
        q, k = qk
        r = rows(s)
        for pair in range(n_q // (2 * LANES)):
            qn = _head_norm_mxu(q[:, 2 * pair * LANES:2 * (pair + 1) * LANES], qg_ref[...], ones_ref[...])
            for half in range(2):
                cols = slice((2 * pair + half) * LANES, (2 * pair + half + 1) * LANES)
                qh = qn[:, half * LANES:(half + 1) * LANES]
                if latent:
                    qh = _rope(qh, cos_ref[r, :], sin_ref[r, :])
                q_ref[r, cols] = (qh * Q_SCALE).astype(BF16)
        kn = _head_norm_mxu(k, kg_ref[...], ones_ref[...])
        for blk in range(n_kv // LANES):
            cols = slice(blk * LANES, (blk + 1) * LANES)
            kh = kn[:, cols]
            if latent:
                kh = _rope(kh, cos_ref[r, :], sin_ref[r, :])
            else:
                k_ref[r, cols] = kh
            for half in range(2):
                dst = slice((2 * blk + half) * LANES, (2 * blk + half + 1) * LANES)
                kd_ref[r, dst] = _dup_half(kh, half).astype(BF16)

    _staggered(x_ref.shape[0] // SUB_TILE, prologue, matmuls, epilogue)


def _odd_in(x2d, mrows, tokens_per_row, ng, w_in, qg, kg, rope):
    n = x2d.shape[0]
    tm = IN_TILE
    latent = rope is not None
    n_q = WIN_HEADS * WIN_HD
    n_kv = WIN_KV * WIN_HD
    in_specs = [
        _tok_spec(D_MODEL, tm), _mod_spec(tm, tokens_per_row), _resident((1, D_MODEL)),
        _resident((D_MODEL, ODD_IN)), _resident((1, 2 * LANES)), _resident((1, 2 * LANES)),
        _resident((2 * LANES, 2 * LANES)),
    ]
    args = [x2d, mrows, ng, w_in, qg, kg, _head_ones()]
    out_specs = [_tok_spec(n_q, tm), _tok_spec(2 * n_kv, tm)]
    out_shape = [jax.ShapeDtypeStruct((n, n_q), BF16), jax.ShapeDtypeStruct((n, 2 * n_kv), BF16)]
    if latent:
        seq = rope[0].shape[0]
        tab = pl.BlockSpec((tm, LANES), lambda i: (i % (seq // tm), 0))
        in_specs += [tab, tab]
        args += list(rope)
        out_specs += [_tok_spec(n_kv, tm)]
        out_shape += [jax.ShapeDtypeStruct((n, n_kv), BF16)]
    else:
        out_specs += [_tok_spec(n_kv, tm)] * 2
        out_shape += [jax.ShapeDtypeStruct((n, n_kv), F32)] * 2
    return pl.pallas_call(
        functools.partial(_odd_in_kernel, latent=latent),
        grid=(n // tm,),
        in_specs=in_specs,
        out_specs=out_specs,
        out_shape=out_shape,
        compiler_params=_cparams("arbitrary"),
        name="odd_in",
    )(*args)


def _win_attn_kernel(*refs, latent, seq, tq):
    if latent:
        q_ref, sink_ref, kd_ref, v_ref, ck_ref, cv_ref, o_ref, vt_scr, ckd_scr, cvt_scr = refs
    else:
        q_ref, sink_ref, kd_ref, v_ref, o_ref, vt_scr = refs
    lo = _lo_mask()
    zero = jnp.zeros((), BF16)
    n_cols = WIN_G * tq
    n_kv = WIN_KV * WIN_HD

    @pl.when(pl.program_id(1) == 0)
    def _():
        srcs = [(v_ref, vt_scr)] + ([(cv_ref, cvt_scr)] if latent else [])
        for src_ref, dst_scr in srcs:
            for blk in range(n_kv // LANES):
                vt2 = src_ref[:, blk * LANES:(blk + 1) * LANES].astype(F32).T.astype(BF16)
                for half in range(2):
                    dst_scr[2 * blk + half, 0:WIN_HD, :] = vt2[half * WIN_HD:(half + 1) * WIN_HD, :]
            for j in range(WIN_KV):
                dst_scr[j, WIN_HD:, :] = jnp.ones((ONES_ROWS, src_ref.shape[0]), BF16)
        if latent:
            for j in range(WIN_KV):
                cblk = slice((j // 2) * LANES, (j // 2 + 1) * LANES)
                ckd_scr[:, j * LANES:(j + 1) * LANES] = _dup_half(ck_ref[:, cblk], j % 2).astype(BF16)

    if latent:
        span = 3 * tq
        i = pl.program_id(1)
        start = pl.multiple_of(jnp.clip((i - 1) * tq, 0, seq - span), tq)
        kpos = start + lax.broadcasted_iota(jnp.int32, (span, 1), 0)
        qpos = i * tq + (lax.broadcasted_iota(jnp.int32, (1, n_cols), 1) & (tq - 1))
        bias = jnp.where(jnp.abs(kpos - qpos) <= WINDOW, 0.0, NEG_INF)

    def scores(j):
        kcols = slice(j * LANES, (j + 1) * LANES)
        qs = []
        for pair in range(WIN_G // 2):
            qb = q_ref[:, (2 * j + pair) * LANES:(2 * j + pair + 1) * LANES]
            qs += [jnp.where(lo, qb, zero), jnp.where(lo, zero, qb)]
        qs = jnp.concatenate(qs, axis=0)
        if latent:
            return [_dot_nt(kd_ref[pl.ds(start, span), kcols], qs) + bias, _dot_nt(ckd_scr[:, kcols], qs)]
        return [_dot_nt(kd_ref[:, kcols], qs)]

    sts_next = scores(0)
    for j in range(WIN_KV):
        sts = sts_next
        if j + 1 < WIN_KV:
            sts_next = scores(j + 1)
        sink = jnp.concatenate([jnp.full((1, tq), sink_ref[WIN_G * j + g] * LOG2E, F32) for g in range(WIN_G)],
                               axis=1)
        vts = [vt_scr[j, :, pl.ds(start, span)], cvt_scr[j]] if latent else [vt_scr[j]]
        m = functools.reduce(jnp.maximum, [jnp.max(st, axis=0, keepdims=True) for st in sts])
        m = jnp.maximum(m, sink)
        ps = [jnp.exp2(st - m) for st in sts]
        ot = functools.reduce(jnp.add, [_dot(vt, p.astype(BF16)) for vt, p in zip(vts, ps)])
        den = ot[WIN_HD:WIN_HD + 1, :] + jnp.exp2(sink - m)
        ot = ot[0:WIN_HD, :] * (1.0 / den)
        for pair in range(WIN_G // 2):
            blk = 2 * j + pair
            c0 = 2 * pair * tq
            both = jnp.concatenate([ot[:, c0:c0 + tq], ot[:, c0 + tq:c0 + 2 * tq]], axis=0)
            o_ref[:, blk * LANES:(blk + 1) * LANES] = both.T.astype(o_ref.dtype)


def _win_attn(q, sink, kd, v, ctx):
    b, seq, n_q = q.shape
    n_kv = v.shape[2]
    latent = ctx is not None
    tq = WIN_Q_TILE if latent else min(seq, 2 * WIN_Q_TILE)
    q_spec = pl.BlockSpec((None, tq, n_q), lambda i, j: (i, j, 0))
    kd_spec = pl.BlockSpec((None, seq, kd.shape[2]), lambda i, j: (i, 0, 0))
    v_spec = pl.BlockSpec((None, seq, n_kv), lambda i, j: (i, 0, 0))
    in_specs = [q_spec, pl.BlockSpec(memory_space=pltpu.SMEM), kd_spec, v_spec]
    args = [q, sink, kd, v]
    scratch = [pltpu.VMEM((WIN_KV, WIN_HD + ONES_ROWS, seq), BF16)]
    if latent:
        ck, cv = ctx
        past = ck.shape[1]
        c_spec = pl.BlockSpec((None, past, n_kv), lambda i, j: (i, 0, 0))
        in_specs += [c_spec, c_spec]
        args += [ck, cv]
        scratch += [pltpu.VMEM((past, kd.shape[2]), BF16), pltpu.VMEM((WIN_KV, WIN_HD + ONES_ROWS, past), BF16)]
    return pl.pallas_call(
        functools.partial(_win_attn_kernel, latent=latent, seq=seq, tq=tq),
        grid=(b, seq // tq),
        in_specs=in_specs,
        out_specs=q_spec,
        out_shape=jax.ShapeDtypeStruct((b, seq, n_q), BF16),
        scratch_shapes=scratch,
        compiler_params=_cparams("arbitrary", "arbitrary"),
        name="win_attn",
    )(*args)


def _rope_tables(seq):
    rows = seq // GRID_W
    row = np.repeat(np.arange(rows), GRID_W).astype(np.float32)
    col = np.tile(np.arange(GRID_W), rows).astype(np.float32)
    half = DIFF_HD // 2
    inv = (ROPE_THETA ** (-np.arange(0, half, 2, dtype=np.float32) / half)).astype(np.float32)
    ang_r = row[:, None] * inv[None, :]
    ang_c = col[:, None] * inv[None, :]
    cos = np.concatenate([np.cos(ang_r), np.cos(ang_r), np.cos(ang_c), np.cos(ang_c)], axis=1)
    sin = np.concatenate([-np.sin(ang_r), np.sin(ang_r), -np.sin(ang_c), np.sin(ang_c)], axis=1)
    cos = np.tile(cos, (1, LANES // DIFF_HD)).astype(np.float32)
    sin = np.tile(sin, (1, LANES // DIFF_HD)).astype(np.float32)
    return jnp.asarray(cos), jnp.asarray(sin)


def _block_diag(w):
    eye = jnp.eye(RNN_BLOCKS, dtype=w.dtype)
    return jnp.einsum('nkj,nm->nkmj', w, eye).reshape(D_RNN, D_RNN)


def _gate_params(wa, ba, wi, bi):
    mats = [_block_diag(wa[0]), _block_diag(wi[0]), _block_diag(wa[1]), _block_diag(wi[1])]
    vecs = [ba[0], bi[0], ba[1], bi[1]]
    w_blocks, b_blocks = [], []
    for blk in range(D_RNN // LANES):
        sl = slice(blk * LANES, (blk + 1) * LANES)
        w_blocks.append(jnp.concatenate([m[sl, sl] for m in mats], axis=1))
        b_blocks.append(jnp.concatenate([v[sl] for v in vecs]).reshape(1, -1))
    return (0.5 * jnp.stack(w_blocks)).astype(BF16), 0.5 * jnp.stack(b_blocks)


def _tile_gain(g, width=LANES):
    return jnp.tile(g, width // g.shape[0]).reshape(1, width)


def _head_ones():
    head = np.arange(2 * LANES) // WIN_HD
    return jnp.asarray((head[:, None] == head[None, :]).astype(np.float32), dtype=BF16)


def _diff_lambda_init(layer):
    return 0.8 - 0.6 * math.exp(-0.3 * layer)


def kernel(x_prompt, x_sample, cache_diff_k, cache_diff_v, state_lru, cache_win_k, cache_win_v, c, c_ctx,
           norm_g, w_mod, b_mod, ffn_w1, ffn_w3, ffn_w2, e_w_in, e_w_out, e_conv_w, e_conv_b,
           e_lru_wa, e_lru_ba, e_lru_wi, e_lru_bi, e_lru_lam, e_q_g, e_k_g, e_lam, e_subln_g,
           o_w_in, o_w_out, o_q_g, o_k_g, o_sink):
    batch, seq, _ = x_prompt.shape
    dec_batch, dec_seq, _ = x_sample.shape
    past = cache_diff_k.shape[2]

    cond = jnp.concatenate([c_ctx[None, :], c], axis=0)
    cond = jnp.pad(cond, ((0, COND_ROWS - cond.shape[0]), (0, 0)))
    mod = _modulation(cond, w_mod, b_mod).reshape(DEPTH, COND_ROWS, N_MOD, D_MODEL)

    w1 = ffn_w1.astype(BF16)
    w3 = ffn_w3.astype(BF16)
    w2 = ffn_w2.astype(BF16)
    rope = _rope_tables(dec_seq)

    groups = [
        dict(x=x_prompt.reshape(batch * seq, D_MODEL), b=batch, s=seq, latent=False,
             rows=slice(0, 1), per_row=batch * seq),
        dict(x=x_sample.reshape(dec_batch * dec_seq, D_MODEL), b=dec_batch, s=dec_seq, latent=True,
             rows=slice(1, 1 + dec_batch), per_row=dec_seq),
    ]
    ctx_out = {}
    finals = []
    for grp in groups:
        x = grp['x']
        nb, s, latent, per_row = grp['b'], grp['s'], grp['latent'], grp['per_row']
        for l in range(DEPTH):
            j = l // 2
            mrows = mod[l, grp['rows']]
            ng = norm_g[l].reshape(3, 1, D_MODEL)
            x = _ffn(x, mrows, per_row, ng[0], w1, w3, w2, l, 0)
            if l % 2 == 0:
                xr, gt, q, k, v = _even_in(x, mrows, per_row, ng[1], e_w_in[j].astype(BF16),
                                           _tile_gain(e_q_g[j]), _tile_gain(e_k_g[j]), rope if latent else None)
                wg, bg = _gate_params(e_lru_wa[j], e_lru_ba[j], e_lru_wi[j], e_lru_bi[j])
                h0 = state_lru[:, j] if latent else jnp.zeros((nb, 2, D_RNN), F32)
                y_rnn, last = _lru(xr.reshape(nb, s, D_RNN), gt.reshape(nb, s, D_RNN), e_conv_w[j],
                                   e_conv_b[j].reshape(1, D_RNN), wg, bg, e_lru_lam[j], h0)
                k3 = k.reshape(nb, s, DIFF_W)
                v3 = v.reshape(nb, s, DIFF_W)
                srcs = [(k3, v3)]
                if latent:
                    srcs = [(cache_diff_k[:, j].reshape(nb, past, DIFF_W),
                             cache_diff_v[:, j].reshape(nb, past, DIFF_W))] + srcs
                else:
                    ctx_out.setdefault('diff_k', []).append(k3.reshape(nb, s, DIFF_HEADS, 2 * DIFF_HD))
                    ctx_out.setdefault('diff_v', []).append(v3.reshape(nb, s, DIFF_HEADS, 2 * DIFF_HD))
                    ctx_out.setdefault('state', []).append(last)
                o = _diff_attn(q.reshape(nb, s, DIFF_W), e_lam[j], e_subln_g[j].reshape(1, LANES), srcs,
                               _diff_lambda_init(l))
                acts = [y_rnn.reshape(nb * s, D_RNN), o.reshape(nb * s, DIFF_W)]
                w_out = e_w_out[j].astype(BF16)
            else:
                outs = _odd_in(x, mrows, per_row, ng[1], o_w_in[j].astype(BF16),
                               _tile_gain(o_q_g[j], 2 * LANES), _tile_gain(o_k_g[j], 2 * LANES),
                               rope if latent else None)
                q, kd = outs[:2]
                v = outs[-1]
                n_kv = WIN_KV * WIN_HD
                ctx = None
                if latent:
                    ctx = (cache_win_k[:, j].reshape(nb, past, n_kv), cache_win_v[:, j].reshape(nb, past, n_kv))
                else:
                    ctx_out.setdefault('win_k', []).append(outs[2].reshape(nb, s, WIN_KV, WIN_HD))
                    ctx_out.setdefault('win_v', []).append(v.reshape(nb, s, WIN_KV, WIN_HD))
                o = _win_attn(q.reshape(nb, s, ODD_MIX), o_sink[j], kd.reshape(nb, s, 2 * n_kv),
                              v.reshape(nb, s, n_kv), ctx)
                acts = [o.reshape(nb * s, ODD_MIX)]
                w_out = o_w_out[j].astype(BF16)
            x = _ffn(x, mrows, per_row, ng[2], w1, w3, w2, l, 1, acts, w_out)
        finals.append(x.reshape(nb, s, D_MODEL))

    return (finals[0], finals[1],
            jnp.stack(ctx_out['diff_k'], axis=1), jnp.stack(ctx_out['diff_v'], axis=1),
            jnp.stack(ctx_out['state'], axis=1),
            jnp.stack(ctx_out['win_k'], axis=1), jnp.stack(ctx_out['win_v'], axis=1))
```

```python
import functools
import math

import numpy as np
import jax
import jax.numpy as jnp
from jax import lax
from jax.experimental import pallas as pl
from jax.experimental.pallas import tpu as pltpu

F32 = jnp.float32
BF16 = jnp.bfloat16

D_MODEL = 1024
DEPTH = 2
N_MOD = 9
D_FF = 2816
GRID_W = 64
ROPE_THETA = 10000.0
EPS = 1e-6
NEG_INF = -1e30

D_RNN = 512
RNN_BLOCKS = 8
RNN_BW = D_RNN // RNN_BLOCKS
LRU_C = 8.0

DIFF_HEADS = 4
DIFF_HD = 64
DIFF_W = DIFF_HEADS * 2 * DIFF_HD

WIN_HEADS = 16
WIN_KV = 4
WIN_G = WIN_HEADS // WIN_KV
WIN_HD = 64
WINDOW = 128

EVEN_IN = 2 * D_RNN + 3 * DIFF_W
ODD_IN = (WIN_HEADS + 2 * WIN_KV) * WIN_HD
ODD_MIX = WIN_HEADS * WIN_HD

LANES = 128
SUBLANES = 8
HEAD_HALF = LANES // 2
ROPE_PAIR = DIFF_HD // 4
ONES_ROWS = 16
LOG2E = math.log2(math.e)
Q_SCALE = DIFF_HD ** -0.5 * LOG2E

TOKEN_TILE = 512
FFN_TILE = 1024
IN_TILE = 1024
SUB_TILE = 256
FF_CHUNK = 256
MOD_COLS = 1152
COND_ROWS = 16
LRU_STEPS = 64
LRU_DOUBLE_BUFFER_MAX = 4 * 1024 * 1024
DIFF_Q_TILE = 512
DIFF_KEY_CHUNK = 512
DIFF_LOOKAHEAD = 2
WIN_Q_TILE = 128
WIN_TILES_PER_STEP = 4
WIN_LOOKAHEAD = 2
VMEM_LIMIT = 56 * 1024 * 1024


def _cparams(*sem):
    return pltpu.CompilerParams(dimension_semantics=sem, vmem_limit_bytes=VMEM_LIMIT)


def _resident(shape):
    return pl.BlockSpec(shape, lambda *_: (0,) * len(shape), pipeline_mode=pl.Buffered(1))


def _dot(a, b):
    return jnp.dot(a, b, preferred_element_type=F32)


def _dot_nt(a, b):
    return lax.dot_general(a, b, (((1,), (1,)), ((), ())), preferred_element_type=F32)


def _rms_mod(x, ng, shift, scale):
    ms = jnp.mean(x * x, axis=-1, keepdims=True)
    y = x * lax.rsqrt(ms + EPS) * ng
    return y * (1.0 + scale) + shift


def _silu(a):
    return a * jax.nn.sigmoid(a)


def _lo_mask():
    return lax.broadcasted_iota(jnp.int32, (1, LANES), 1) < HEAD_HALF


def _head_norm(t, g):
    lo = _lo_mask()
    sq = t * t
    s_lo = jnp.sum(jnp.where(lo, sq, 0.0), axis=-1, keepdims=True)
    s_hi = jnp.sum(jnp.where(lo, 0.0, sq), axis=-1, keepdims=True)
    inv = jnp.where(lo, lax.rsqrt(s_lo / DIFF_HD + EPS), lax.rsqrt(s_hi / DIFF_HD + EPS))
    return t * inv * g


def _head_norm_mxu(t, g, head_ones):
    sq = t * t
    hi = sq.astype(BF16)
    lo = (sq - hi.astype(F32)).astype(BF16)
    ss = _dot(hi, head_ones) + _dot(lo, head_ones)
    return t * lax.rsqrt(ss / DIFF_HD + EPS) * g


def _rope(t, cos, sin_signed):
    lane = lax.broadcasted_iota(jnp.int32, (1, LANES), 1)
    first = (lane % (2 * ROPE_PAIR)) < ROPE_PAIR
    partner = jnp.where(first, pltpu.roll(t, LANES - ROPE_PAIR, axis=1), pltpu.roll(t, ROPE_PAIR, axis=1))
    return t * cos + partner * sin_signed


def _swap_halves(t):
    return pltpu.roll(t, HEAD_HALF, axis=1)


def _dup_half(t, which):
    lo = _lo_mask()
    r = _swap_halves(t)
    return jnp.where(lo, t, r) if which == 0 else jnp.where(lo, r, t)


def _staggered(n_sub, prologue, matmuls, epilogue):
    h = prologue(0)
    pending = None
    for s in range(n_sub):
        d = matmuls(s, h)
        if s + 1 < n_sub:
            h = prologue(s + 1)
        if pending is not None:
            epilogue(*pending)
        pending = (s, d)
    epilogue(*pending)


def _mod_kernel(c_ref, w_ref, b_ref, o_ref):
    c = c_ref[...]
    s = _silu(c).astype(BF16)
    o_ref[...] = _dot(s, w_ref[...].astype(BF16)) + b_ref[...]


def _modulation(cond, w_mod, b_mod):
    n_col = N_MOD * D_MODEL
    return pl.pallas_call(
        _mod_kernel,
        grid=(DEPTH, n_col // MOD_COLS),
        in_specs=[
            pl.BlockSpec((COND_ROWS, D_MODEL), lambda l, j: (0, 0)),
            pl.BlockSpec((None, D_MODEL, MOD_COLS), lambda l, j: (l, 0, j)),
            pl.BlockSpec((None, 1, MOD_COLS), lambda l, j: (l, 0, j)),
        ],
        out_specs=pl.BlockSpec((None, COND_ROWS, MOD_COLS), lambda l, j: (l, 0, j)),
        out_shape=jax.ShapeDtypeStruct((DEPTH, COND_ROWS, n_col), F32),
        compiler_params=_cparams("arbitrary", "arbitrary"),
        name="modulation",
    )(cond, w_mod, b_mod.reshape(DEPTH, 1, n_col))


def _tok_spec(width, tm):
    return pl.BlockSpec((tm, width), lambda i: (i, 0))


def _mod_spec(tm, tokens_per_row):
    return pl.BlockSpec((None, N_MOD, D_MODEL), lambda i: ((i * tm) // tokens_per_row, 0, 0))


def _ffn_kernel(*refs, mi, n_act):
    x_ref, m_ref, ng_ref, w1_ref, w3_ref, w2_ref = refs[:6]
    acts = refs[6:6 + n_act]
    rest = refs[6 + n_act:]
    if n_act:
        wo_ref, o_ref, x_scr, g_scr = rest
    else:
        o_ref, x_scr, g_scr = rest

    def rows(s):
        return slice(s * TOKEN_TILE, (s + 1) * TOKEN_TILE)

    def prologue(s):
        x = x_ref[rows(s), :]
        if n_act:
            return x
        x_scr[s] = x
        return _rms_mod(x, ng_ref[...], m_ref[mi:mi + 1, :], m_ref[mi + 1:mi + 2, :]).astype(BF16)

    def matmuls(s, h):
        if n_act:
            y = None
            row = 0
            for a_ref in acts:
                ka = a_ref.shape[1]
                part = _dot(a_ref[rows(s), :], wo_ref[row:row + ka, :])
                y = part if y is None else y + part
                row += ka
            x = h + m_ref[5:6, :] * y
            x_scr[s] = x
            h = _rms_mod(x, ng_ref[...], m_ref[mi:mi + 1, :], m_ref[mi + 1:mi + 2, :]).astype(BF16)
        for j in range(D_FF // FF_CHUNK):
            cols = slice(j * FF_CHUNK, (j + 1) * FF_CHUNK)
            a = _dot(h, w1_ref[:, cols])
            b = _dot(h, w3_ref[:, cols])
            g_scr[:, cols] = (_silu(a) * b).astype(BF16)
        return _dot(g_scr[...], w2_ref[...])

    def epilogue(s, y):
        o_ref[rows(s), :] = x_scr[s] + 0.5 * m_ref[mi + 2:mi + 3, :] * y

    _staggered(x_ref.shape[0] // TOKEN_TILE, prologue, matmuls, epilogue)


def _ffn(x2d, mrows, tokens_per_row, ng, w1, w3, w2, layer, which, acts=(), w_out=None):
    n = x2d.shape[0]
    tm = FFN_TILE
    n_sub = tm // TOKEN_TILE
    mi = 6 * which

    def stack_spec(rows_, cols_):
        return pl.BlockSpec((None, None, rows_, cols_), lambda i: (layer, which, 0, 0), pipeline_mode=pl.Buffered(1))

    in_specs = [
        _tok_spec(D_MODEL, tm),
        _mod_spec(tm, tokens_per_row),
        _resident((1, D_MODEL)),
        stack_spec(D_MODEL, D_FF),
        stack_spec(D_MODEL, D_FF),
        stack_spec(D_FF, D_MODEL),
    ] + [_tok_spec(a.shape[1], tm) for a in acts]
    args = [x2d, mrows, ng, w1, w3, w2, *acts]
    if acts:
        in_specs.append(_resident(w_out.shape))
        args.append(w_out)
    return pl.pallas_call(
        functools.partial(_ffn_kernel, mi=mi, n_act=len(acts)),
        grid=(n // tm,),
        in_specs=in_specs,
        out_specs=_tok_spec(D_MODEL, tm),
        out_shape=jax.ShapeDtypeStruct((n, D_MODEL), F32),
        scratch_shapes=[pltpu.VMEM((n_sub, TOKEN_TILE, D_MODEL), F32), pltpu.VMEM((TOKEN_TILE, D_FF), BF16)],
        compiler_params=_cparams("arbitrary"),
        name="swiglu",
    )(*args)


def _even_in_kernel(*refs, latent):
    if latent:
        x_ref, m_ref, ng_ref, w_ref, qg_ref, kg_ref, cos_ref, sin_ref = refs[:8]
        outs = refs[8:]
    else:
        x_ref, m_ref, ng_ref, w_ref, qg_ref, kg_ref = refs[:6]
        outs = refs[6:]
    xr_ref, gt_ref, q_ref, k_ref, v_ref = outs
    base = 2 * D_RNN

    def rows(s):
        return slice(s * SUB_TILE, (s + 1) * SUB_TILE)

    def prologue(s):
        return _rms_mod(x_ref[rows(s), :], ng_ref[...], m_ref[3:4, :], m_ref[4:5, :]).astype(BF16)

    def matmuls(s, h):
        r = rows(s)
        xr_ref[r, :] = _dot(h, w_ref[:, 0:D_RNN])
        gt_ref[r, :] = _dot(h, w_ref[:, D_RNN:2 * D_RNN])
        v_ref[r, :] = _dot(h, w_ref[:, base + 2 * DIFF_W:base + 3 * DIFF_W]).astype(v_ref.dtype)
        return _dot(h, w_ref[:, base:base + DIFF_W]), _dot(h, w_ref[:, base + DIFF_W:base + 2 * DIFF_W])

    def epilogue(s, qk):
        q, k = qk
        r = rows(s)
        for hh in range(DIFF_HEADS):
            cols = slice(hh * LANES, (hh + 1) * LANES)
            qh = _head_norm(q[:, cols], qg_ref[...])
            kh = _head_norm(k[:, cols], kg_ref[...])
            if latent:
                qh = _rope(qh, cos_ref[r, :], sin_ref[r, :])
                kh = _rope(kh, cos_ref[r, :], sin_ref[r, :])
            q_ref[r, cols] = (qh * Q_SCALE).astype(BF16)
            k_ref[r, cols] = kh.astype(k_ref.dtype)

    _staggered(x_ref.shape[0] // SUB_TILE, prologue, matmuls, epilogue)


def _even_in(x2d, mrows, tokens_per_row, ng, w_in, qg, kg, rope):
    n = x2d.shape[0]
    tm = IN_TILE
    latent = rope is not None
    in_specs = [
        _tok_spec(D_MODEL, tm), _mod_spec(tm, tokens_per_row), _resident((1, D_MODEL)),
        _resident((D_MODEL, EVEN_IN)), _resident((1, LANES)), _resident((1, LANES)),
    ]
    args = [x2d, mrows, ng, w_in, qg, kg]
    if latent:
        seq = rope[0].shape[0]
        tab = pl.BlockSpec((tm, LANES), lambda i: (i % (seq // tm), 0))
        in_specs += [tab, tab]
        args += list(rope)
    kv_dtype = BF16 if latent else F32
    return pl.pallas_call(
        functools.partial(_even_in_kernel, latent=latent),
        grid=(n // tm,),
        in_specs=in_specs,
        out_specs=[_tok_spec(D_RNN, tm)] * 2 + [_tok_spec(DIFF_W, tm)] * 3,
        out_shape=[
            jax.ShapeDtypeStruct((n, D_RNN), F32), jax.ShapeDtypeStruct((n, D_RNN), F32),
            jax.ShapeDtypeStruct((n, DIFF_W), BF16),
            jax.ShapeDtypeStruct((n, DIFF_W), kv_dtype), jax.ShapeDtypeStruct((n, DIFF_W), kv_dtype),
        ],
        compiler_params=_cparams("arbitrary"),
        name="even_in",
    )(*args)


def _gelu_tanh(x):
    return 0.5 * x * (1.0 + jnp.tanh(math.sqrt(2.0 / math.pi) * (x + 0.044715 * (x * x * x))))


def _lru_kernel(xr_ref, gt_ref, cw_ref, cb_ref, wg_ref, bg_ref, lam_ref, h0_ref,
                y_ref, last_ref, hf_scr, hb_scr, af0, uf0, ab0, ub0, af1, uf1, ab1, ub1, *, seq, tt):
    n_chunks = seq // tt
    rows = tt * SUBLANES
    lam = lam_ref[...]
    m2sp = (-0.25 * LRU_C) * (jnp.maximum(-lam, 0.0) + jnp.log1p(jnp.exp(-jnp.abs(lam))))

    def time_major(ref, t0, n):
        return jnp.swapaxes(ref[:, pl.ds(pl.multiple_of(t0, SUBLANES), n), :], 0, 1)

    def conv_chunk(c):
        t0 = c * tt
        before = time_major(xr_ref, jnp.maximum(t0 - SUBLANES, 0), SUBLANES)[SUBLANES - 2:]
        after = time_major(xr_ref, jnp.minimum(t0 + tt, seq - SUBLANES), SUBLANES)[:1]
        before = jnp.where(c > 0, before, 0.0)
        after = jnp.where(c < n_chunks - 1, after, 0.0)
        xw = jnp.concatenate([before, time_major(xr_ref, t0, tt), after], axis=0)
        xc = cb_ref[...]
        for tap in range(4):
            xc = xc + xw[tap:tap + tt] * cw_ref[tap:tap + 1, :]
        return xc.reshape(rows, LANES)

    def decay_and_input(th_r, th_i, xc, m2sp_row):
        t = jnp.tanh(m2sp_row * th_r + m2sp_row)
        inv = 1.0 / (1.0 - t)
        a = (1.0 + t) * inv
        nt = -t
        root = jnp.where(nt > 0.0, nt * lax.rsqrt(nt), 0.0)
        u = (root * inv) * ((th_i + 1.0) * xc)
        return a.reshape(tt, SUBLANES, LANES), u.reshape(tt, SUBLANES, LANES)

    def gates(c, d):
        xc = conv_chunk(c)
        cols = slice(2 * d * LANES, 2 * (d + 1) * LANES)
        th = jnp.tanh(_dot(xc.astype(BF16), wg_ref[:, cols]) + bg_ref[:, cols])
        return decay_and_input(th[:, :LANES], th[:, LANES:], xc, m2sp[d:d + 1, :])

    bufs = ((af0, uf0, ab0, ub0), (af1, uf1, ab1, ub1))

    def fill(c, buf):
        buf[0][...], buf[1][...] = gates(c, 0)
        buf[2][...], buf[3][...] = gates(n_chunks - 1 - c, 1)

    def scan(c, buf, carry):
        af, uf, ab, ub = buf
        hf, hb = carry
        f0 = c * tt
        b0 = (n_chunks - 1 - c) * tt
        for i in range(tt):
            hf = af[i] * hf + uf[i]
            hf_scr[f0 + i] = hf
            ib = tt - 1 - i
            hb = ab[ib] * hb + ub[ib]
            hb_scr[b0 + ib] = hb
        return hf, hb

    def two_trips(k, carry):
        c = 2 * k
        fill(c + 1, bufs[1])
        carry = scan(c, bufs[0], carry)
        fill(jnp.minimum(c + 2, n_chunks - 1), bufs[0])
        return scan(c + 1, bufs[1], carry)

    fill(0, bufs[0])
    hf, hb = lax.fori_loop(0, n_chunks // 2, two_trips, (h0_ref[:, 0, :], h0_ref[:, 1, :]))
    last_ref[:, 0, :] = hf
    last_ref[:, 1, :] = hb

    def finish(c, _):
        t0 = pl.multiple_of(c * tt, tt)
        y = (hf_scr[pl.ds(t0, tt)] + hb_scr[pl.ds(t0, tt)]) * _gelu_tanh(time_major(gt_ref, t0, tt))
        y_ref[:, pl.ds(t0, tt), :] = jnp.swapaxes(y, 0, 1).astype(y_ref.dtype)
        return 0

    lax.fori_loop(0, n_chunks, finish, 0)


def _lru(xr, gt, conv_w, conv_b, wg, bg, lam, h0):
    b, seq, _ = xr.shape
    tt = min(LRU_STEPS, seq)
    blk_bytes = seq * SUBLANES * LANES * 4
    mode = dict(pipeline_mode=pl.Buffered(1)) if blk_bytes > LRU_DOUBLE_BUFFER_MAX else {}
    seq_in = pl.BlockSpec((SUBLANES, seq, LANES), lambda i, j: (i, 0, j), **mode)
    seq_out = pl.BlockSpec((SUBLANES, seq, LANES), lambda i, j: (i, 0, j))
    st_spec = pl.BlockSpec((SUBLANES, 2, LANES), lambda i, j: (i, 0, j))
    return pl.pallas_call(
        functools.partial(_lru_kernel, seq=seq, tt=tt),
        grid=(b // SUBLANES, D_RNN // LANES),
        in_specs=[seq_in, seq_in,
                  pl.BlockSpec((4, LANES), lambda i, j: (0, j)), pl.BlockSpec((1, LANES), lambda i, j: (0, j)),
                  pl.BlockSpec((None, LANES, 4 * LANES), lambda i, j: (j, 0, 0)),
                  pl.BlockSpec((None, 1, 4 * LANES), lambda i, j: (j, 0, 0)),
                  pl.BlockSpec((2, LANES), lambda i, j: (0, j)), st_spec],
        out_specs=[seq_out, st_spec],
        out_shape=[jax.ShapeDtypeStruct((b, seq, D_RNN), BF16), jax.ShapeDtypeStruct((b, 2, D_RNN), F32)],
        scratch_shapes=[pltpu.VMEM((seq, SUBLANES, LANES), F32)] * 2 + [pltpu.VMEM((tt, SUBLANES, LANES), F32)] * 8,
        compiler_params=_cparams("arbitrary", "arbitrary"),
        name="rglru",
    )(xr, gt, conv_w, conv_b, wg, bg, lam, h0)


def _diff_attn_kernel(*refs, n_src, lam_init):
    q_ref, lamv_ref, sg_ref = refs[:3]
    srcs = [(refs[3 + 2 * s], refs[4 + 2 * s]) for s in range(n_src)]
    o_ref = refs[3 + 2 * n_src]
    scr = refs[4 + 2 * n_src:]
    kb = [scr[2 * s] for s in range(n_src)]
    vt = [scr[2 * s + 1] for s in range(n_src)]

    @pl.when(pl.program_id(1) == 0)
    def _():
        for s, (k_ref, v_ref) in enumerate(srcs):
            kb[s][...] = k_ref[...].astype(BF16)
            for hh in range(DIFF_HEADS):
                cols = slice(hh * LANES, (hh + 1) * LANES)
                vt[s][hh, 0:LANES, :] = v_ref[:, cols].astype(F32).T.astype(BF16)
                vt[s][hh, LANES:, :] = jnp.ones((ONES_ROWS, v_ref.shape[0]), BF16)

    lv = lamv_ref[...]
    lam = (jnp.exp(jnp.sum(lv[0:1, :] * lv[1:2, :], axis=-1, keepdims=True))
           - jnp.exp(jnp.sum(lv[2:3, :] * lv[3:4, :], axis=-1, keepdims=True)) + lam_init)
    lo = _lo_mask()
    zero = jnp.zeros((), BF16)
    tq = q_ref.shape[0]

    chunks = []
    for s, (k_ref, _) in enumerate(srcs):
        t_s = k_ref.shape[0]
        step = min(DIFF_KEY_CHUNK, t_s)
        chunks += [(s, t0, step) for t0 in range(0, t_s, step)]
    stages = [(hh, ci) for hh in range(DIFF_HEADS) for ci in range(len(chunks))]

    def masked_queries(hh):
        qh = q_ref[:, hh * LANES:(hh + 1) * LANES]
        return jnp.concatenate([jnp.where(lo, qh, zero), jnp.where(lo, zero, qh)], axis=0)

    qs = [masked_queries(hh) for hh in range(DIFF_HEADS)]

    def scores(hh, ci):
        s, t0, n = chunks[ci]
        return _dot_nt(kb[s][t0:t0 + n, hh * LANES:(hh + 1) * LANES], qs[hh])

    queue = [scores(*stages[i]) for i in range(min(DIFF_LOOKAHEAD, len(stages)))]
    m = acc = None
    for idx, (hh, ci) in enumerate(stages):
        st = queue.pop(0)
        if idx + DIFF_LOOKAHEAD < len(stages):
            queue.append(scores(*stages[idx + DIFF_LOOKAHEAD]))
        s, t0, n = chunks[ci]
        vt_c = vt[s][hh, :, t0:t0 + n]
        m_c = jnp.max(st, axis=0, keepdims=True)
        if ci == 0:
            m = m_c
            acc = _dot(vt_c, jnp.exp2(st - m).astype(BF16))
        else:
            m_new = jnp.maximum(m, m_c)
            acc = acc * jnp.exp2(m - m_new) + _dot(vt_c, jnp.exp2(st - m_new).astype(BF16))
            m = m_new
        if ci == len(chunks) - 1:
            cols = slice(hh * LANES, (hh + 1) * LANES)
            o2t = acc[0:LANES, :] * (1.0 / acc[LANES:LANES + 1, :])
            o = (o2t[:, :tq] - lam * o2t[:, tq:]).T
            ms = jnp.mean(o * o, axis=-1, keepdims=True)
            o = o * lax.rsqrt(ms + EPS) * sg_ref[...] * (1.0 - lam_init)
            o_ref[:, cols] = o.astype(o_ref.dtype)


def _diff_attn(q, lam_vec, subln_g, srcs, lam_init):
    b, seq, _ = q.shape
    tq = min(DIFF_Q_TILE, seq)
    q_spec = pl.BlockSpec((None, tq, DIFF_W), lambda i, j: (i, j, 0))
    in_specs = [q_spec, _resident((4, DIFF_HD)), _resident((1, LANES))]
    args = [q, lam_vec, subln_g]
    scratch = []
    for k, v in srcs:
        kv_spec = pl.BlockSpec((None, k.shape[1], DIFF_W), lambda i, j: (i, 0, 0))
        in_specs += [kv_spec, kv_spec]
        args += [k, v]
        scratch += [pltpu.VMEM((k.shape[1], DIFF_W), BF16),
                    pltpu.VMEM((DIFF_HEADS, LANES + ONES_ROWS, k.shape[1]), BF16)]
    return pl.pallas_call(
        functools.partial(_diff_attn_kernel, n_src=len(srcs), lam_init=lam_init),
        grid=(b, seq // tq),
        in_specs=in_specs,
        out_specs=q_spec,
        out_shape=jax.ShapeDtypeStruct((b, seq, DIFF_W), BF16),
        scratch_shapes=scratch,
        compiler_params=_cparams("arbitrary", "arbitrary"),
        name="diff_attn",
    )(*args)


def _odd_in_kernel(*refs, latent):
    if latent:
        x_ref, m_ref, ng_ref, w_ref, qg_ref, kg_ref, ones_ref, cos_ref, sin_ref = refs[:9]
        q_ref, kd_ref, v_ref = refs[9:]
    else:
        x_ref, m_ref, ng_ref, w_ref, qg_ref, kg_ref, ones_ref = refs[:7]
        q_ref, kd_ref, k_ref, v_ref = refs[7:]
    n_q = WIN_HEADS * WIN_HD
    n_kv = WIN_KV * WIN_HD

    def rows(s):
        return slice(s * SUB_TILE, (s + 1) * SUB_TILE)

    def prologue(s):
        return _rms_mod(x_ref[rows(s), :], ng_ref[...], m_ref[3:4, :], m_ref[4:5, :]).astype(BF16)

    def matmuls(s, h):
        v_ref[rows(s), :] = _dot(h, w_ref[:, n_q + n_kv:n_q + 2 * n_kv]).astype(v_ref.dtype)
        return _dot(h, w_ref[:, 0:n_q]), _dot(h, w_ref[:, n_q:n_q + n_kv])

    def epilogue(s, qk):
        q, k = qk
        r = rows(s)
        for pair in range(n_q // (2 * LANES)):
            qn = _head_norm_mxu(q[:, 2 * pair * LANES:2 * (pair + 1) * LANES], qg_ref[...], ones_ref[...])
            for half in range(2):
                cols = slice((2 * pair + half) * LANES, (2 * pair + half + 1) * LANES)
                qh = qn[:, half * LANES:(half + 1) * LANES]
                if latent:
                    qh = _rope(qh, cos_ref[r, :], sin_ref[r, :])
                q_ref[r, cols] = (qh * Q_SCALE).astype(BF16)
        kn = _head_norm_mxu(k, kg_ref[...], ones_ref[...])
        for blk in range(n_kv // LANES):
            cols = slice(blk * LANES, (blk + 1) * LANES)
            kh = kn[:, cols]
            if latent:
                kh = _rope(kh, cos_ref[r, :], sin_ref[r, :])
            else:
                k_ref[r, cols] = kh
            for half in range(2):
                dst = slice((2 * blk + half) * LANES, (2 * blk + half + 1) * LANES)
                kd_ref[r, dst] = _dup_half(kh, half).astype(BF16)

    _staggered(x_ref.shape[0] // SUB_TILE, prologue, matmuls, epilogue)


def _odd_in(x2d, mrows, tokens_per_row, ng, w_in, qg, kg, rope):
    n = x2d.shape[0]
    tm = IN_TILE
    latent = rope is not None
    n_q = WIN_HEADS * WIN_HD
    n_kv = WIN_KV * WIN_HD
    in_specs = [
        _tok_spec(D_MODEL, tm), _mod_spec(tm, tokens_per_row), _resident((1, D_MODEL)),
        _resident((D_MODEL, ODD_IN)), _resident((1, 2 * LANES)), _resident((1, 2 * LANES)),
        _resident((2 * LANES, 2 * LANES)),
    ]
    args = [x2d, mrows, ng, w_in, qg, kg, _head_ones()]
    out_specs = [_tok_spec(n_q, tm), _tok_spec(2 * n_kv, tm)]
    out_shape = [jax.ShapeDtypeStruct((n, n_q), BF16), jax.ShapeDtypeStruct((n, 2 * n_kv), BF16)]
    if latent:
        seq = rope[0].shape[0]
        tab = pl.BlockSpec((tm, LANES), lambda i: (i % (seq // tm), 0))
        in_specs += [tab, tab]
        args += list(rope)
        out_specs += [_tok_spec(n_kv, tm)]
        out_shape += [jax.ShapeDtypeStruct((n, n_kv), BF16)]
    else:
        out_specs += [_tok_spec(n_kv, tm)] * 2
        out_shape += [jax.ShapeDtypeStruct((n, n_kv), F32)] * 2
    return pl.pallas_call(
        functools.partial(_odd_in_kernel, latent=latent),
        grid=(n // tm,),
        in_specs=in_specs,
        out_specs=out_specs,
        out_shape=out_shape,
        compiler_params=_cparams("arbitrary"),
        name="odd_in",
    )(*args)


def _win_attn_kernel(*refs, latent, seq, tq):
    if latent:
        q_ref, sink_ref, kd_ref, v_ref, ck_ref, cv_ref, o_ref, vt_scr, ckd_scr, cvt_scr = refs
    else:
        q_ref, sink_ref, kd_ref, v_ref, o_ref, vt_scr = refs
    lo = _lo_mask()
    zero = jnp.zeros((), BF16)
    n_cols = WIN_G * tq
    n_kv = WIN_KV * WIN_HD

    @pl.when(pl.program_id(1) == 0)
    def _():
        srcs = [(v_ref, vt_scr)] + ([(cv_ref, cvt_scr)] if latent else [])
        for src_ref, dst_scr in srcs:
            for blk in range(n_kv // LANES):
                vt2 = src_ref[:, blk * LANES:(blk + 1) * LANES].astype(F32).T.astype(BF16)
                for half in range(2):
                    dst_scr[2 * blk + half, 0:WIN_HD, :] = vt2[half * WIN_HD:(half + 1) * WIN_HD, :]
            for j in range(WIN_KV):
                dst_scr[j, WIN_HD:, :] = jnp.ones((ONES_ROWS, src_ref.shape[0]), BF16)
        if latent:
            for j in range(WIN_KV):
                cblk = slice((j // 2) * LANES, (j // 2 + 1) * LANES)
                ckd_scr[:, j * LANES:(j + 1) * LANES] = _dup_half(ck_ref[:, cblk], j % 2).astype(BF16)

    n_tiles = q_ref.shape[0] // tq
    starts, biases = [], []
    if latent:
        span = 3 * tq
        for t in range(n_tiles):
            i = pl.program_id(1) * n_tiles + t
            start = pl.multiple_of(jnp.clip((i - 1) * tq, 0, seq - span), tq)
            kpos = start + lax.broadcasted_iota(jnp.int32, (span, 1), 0)
            qpos = i * tq + (lax.broadcasted_iota(jnp.int32, (1, n_cols), 1) & (tq - 1))
            starts.append(start)
            biases.append(jnp.where(jnp.abs(kpos - qpos) <= WINDOW, 0.0, NEG_INF))

    def scores(t, j):
        kcols = slice(j * LANES, (j + 1) * LANES)
        qs = []
        for pair in range(WIN_G // 2):
            qb = q_ref[t * tq:(t + 1) * tq, (2 * j + pair) * LANES:(2 * j + pair + 1) * LANES]
            qs += [jnp.where(lo, qb, zero), jnp.where(lo, zero, qb)]
        qs = jnp.concatenate(qs, axis=0)
        if latent:
            return [_dot_nt(kd_ref[pl.ds(starts[t], span), kcols], qs) + biases[t], _dot_nt(ckd_scr[:, kcols], qs)]
        return [_dot_nt(kd_ref[:, kcols], qs)]

    stages = [(t, j) for t in range(n_tiles) for j in range(WIN_KV)]
    queue = [scores(*stages[n]) for n in range(min(WIN_LOOKAHEAD, len(stages)))]
    for n, (t, j) in enumerate(stages):
        sts = queue.pop(0)
        if n + WIN_LOOKAHEAD < len(stages):
            queue.append(scores(*stages[n + WIN_LOOKAHEAD]))
        trows = slice(t * tq, (t + 1) * tq)
        sink = jnp.concatenate([jnp.full((1, tq), sink_ref[WIN_G * j + g] * LOG2E, F32) for g in range(WIN_G)],
                               axis=1)
        vts = [vt_scr[j, :, pl.ds(starts[t], span)], cvt_scr[j]] if latent else [vt_scr[j]]
        m = functools.reduce(jnp.maximum, [jnp.max(st, axis=0, keepdims=True) for st in sts])
        m = jnp.maximum(m, sink)
        ps = [jnp.exp2(st - m) for st in sts]
        ot = functools.reduce(jnp.add, [_dot(vt, p.astype(BF16)) for vt, p in zip(vts, ps)])
        den = ot[WIN_HD:WIN_HD + 1, :] + jnp.exp2(sink - m)
        ot = ot[0:WIN_HD, :] * (1.0 / den)
        for pair in range(WIN_G // 2):
            blk = 2 * j + pair
            c0 = 2 * pair * tq
            both = jnp.concatenate([ot[:, c0:c0 + tq], ot[:, c0 + tq:c0 + 2 * tq]], axis=0)
            o_ref[trows, blk * LANES:(blk + 1) * LANES] = both.T.astype(o_ref.dtype)


def _win_attn(q, sink, kd, v, ctx):
    b, seq, n_q = q.shape
    n_kv = v.shape[2]
    latent = ctx is not None
    tq = WIN_Q_TILE if latent else min(seq, 2 * WIN_Q_TILE)
    step_rows = WIN_TILES_PER_STEP * tq if latent else tq
    q_spec = pl.BlockSpec((None, step_rows, n_q), lambda i, j: (i, j, 0))
    kd_spec = pl.BlockSpec((None, seq, kd.shape[2]), lambda i, j: (i, 0, 0))
    v_spec = pl.BlockSpec((None, seq, n_kv), lambda i, j: (i, 0, 0))
    in_specs = [q_spec, pl.BlockSpec(memory_space=pltpu.SMEM), kd_spec, v_spec]
    args = [q, sink, kd, v]
    scratch = [pltpu.VMEM((WIN_KV, WIN_HD + ONES_ROWS, seq), BF16)]
    if latent:
        ck, cv = ctx
        past = ck.shape[1]
        c_spec = pl.BlockSpec((None, past, n_kv), lambda i, j: (i, 0, 0))
        in_specs += [c_spec, c_spec]
        args += [ck, cv]
        scratch += [pltpu.VMEM((past, kd.shape[2]), BF16), pltpu.VMEM((WIN_KV, WIN_HD + ONES_ROWS, past), BF16)]
    return pl.pallas_call(
        functools.partial(_win_attn_kernel, latent=latent, seq=seq, tq=tq),
        grid=(b, seq // step_rows),
        in_specs=in_specs,
        out_specs=q_spec,
        out_shape=jax.ShapeDtypeStruct((b, seq, n_q), BF16),
        scratch_shapes=scratch,
        compiler_params=_cparams("arbitrary", "arbitrary"),
        name="win_attn",
    )(*args)


def _rope_tables(seq):
    rows = seq // GRID_W
    row = np.repeat(np.arange(rows), GRID_W).astype(np.float32)
    col = np.tile(np.arange(GRID_W), rows).astype(np.float32)
    half = DIFF_HD // 2
    inv = (ROPE_THETA ** (-np.arange(0, half, 2, dtype=np.float32) / half)).astype(np.float32)
    ang_r = row[:, None] * inv[None, :]
    ang_c = col[:, None] * inv[None, :]
    cos = np.concatenate([np.cos(ang_r), np.cos(ang_r), np.cos(ang_c), np.cos(ang_c)], axis=1)
    sin = np.concatenate([-np.sin(ang_r), np.sin(ang_r), -np.sin(ang_c), np.sin(ang_c)], axis=1)
    cos = np.tile(cos, (1, LANES // DIFF_HD)).astype(np.float32)
    sin = np.tile(sin, (1, LANES // DIFF_HD)).astype(np.float32)
    return jnp.asarray(cos), jnp.asarray(sin)


def _block_diag(w):
    eye = jnp.eye(RNN_BLOCKS, dtype=w.dtype)
    return jnp.einsum('nkj,nm->nkmj', w, eye).reshape(D_RNN, D_RNN)


def _gate_params(wa, ba, wi, bi):
    mats = [_block_diag(wa[0]), _block_diag(wi[0]), _block_diag(wa[1]), _block_diag(wi[1])]
    vecs = [ba[0], bi[0], ba[1], bi[1]]
    w_blocks, b_blocks = [], []
    for blk in range(D_RNN // LANES):
        sl = slice(blk * LANES, (blk + 1) * LANES)
        w_blocks.append(jnp.concatenate([m[sl, sl] for m in mats], axis=1))
        b_blocks.append(jnp.concatenate([v[sl] for v in vecs]).reshape(1, -1))
    return (0.5 * jnp.stack(w_blocks)).astype(BF16), 0.5 * jnp.stack(b_blocks)


def _tile_gain(g, width=LANES):
    return jnp.tile(g, width // g.shape[0]).reshape(1, width)


def _head_ones():
    head = np.arange(2 * LANES) // WIN_HD
    return jnp.asarray((head[:, None] == head[None, :]).astype(np.float32), dtype=BF16)


def _diff_lambda_init(layer):
    return 0.8 - 0.6 * math.exp(-0.3 * layer)


def kernel(x_prompt, x_sample, cache_diff_k, cache_diff_v, state_lru, cache_win_k, cache_win_v, c, c_ctx,
           norm_g, w_mod, b_mod, ffn_w1, ffn_w3, ffn_w2, e_w_in, e_w_out, e_conv_w, e_conv_b,
           e_lru_wa, e_lru_ba, e_lru_wi, e_lru_bi, e_lru_lam, e_q_g, e_k_g, e_lam, e_subln_g,
           o_w_in, o_w_out, o_q_g, o_k_g, o_sink):
    batch, seq, _ = x_prompt.shape
    dec_batch, dec_seq, _ = x_sample.shape
    past = cache_diff_k.shape[2]

    cond = jnp.concatenate([c_ctx[None, :], c], axis=0)
    cond = jnp.pad(cond, ((0, COND_ROWS - cond.shape[0]), (0, 0)))
    mod = _modulation(cond, w_mod, b_mod).reshape(DEPTH, COND_ROWS, N_MOD, D_MODEL)

    w1 = ffn_w1.astype(BF16)
    w3 = ffn_w3.astype(BF16)
    w2 = ffn_w2.astype(BF16)
    rope = _rope_tables(dec_seq)

    groups = [
        dict(x=x_prompt.reshape(batch * seq, D_MODEL), b=batch, s=seq, latent=False,
             rows=slice(0, 1), per_row=batch * seq),
        dict(x=x_sample.reshape(dec_batch * dec_seq, D_MODEL), b=dec_batch, s=dec_seq, latent=True,
             rows=slice(1, 1 + dec_batch), per_row=dec_seq),
    ]
    ctx_out = {}
    finals = []
    for grp in groups:
        x = grp['x']
        nb, s, latent, per_row = grp['b'], grp['s'], grp['latent'], grp['per_row']
        for l in range(DEPTH):
            j = l // 2
            mrows = mod[l, grp['rows']]
            ng = norm_g[l].reshape(3, 1, D_MODEL)
            x = _ffn(x, mrows, per_row, ng[0], w1, w3, w2, l, 0)
            if l % 2 == 0:
                xr, gt, q, k, v = _even_in(x, mrows, per_row, ng[1], e_w_in[j].astype(BF16),
                                           _tile_gain(e_q_g[j]), _tile_gain(e_k_g[j]), rope if latent else None)
                wg, bg = _gate_params(e_lru_wa[j], e_lru_ba[j], e_lru_wi[j], e_lru_bi[j])
                h0 = state_lru[:, j] if latent else jnp.zeros((nb, 2, D_RNN), F32)
                y_rnn, last = _lru(xr.reshape(nb, s, D_RNN), gt.reshape(nb, s, D_RNN), e_conv_w[j],
                                   e_conv_b[j].reshape(1, D_RNN), wg, bg, e_lru_lam[j], h0)
                k3 = k.reshape(nb, s, DIFF_W)
                v3 = v.reshape(nb, s, DIFF_W)
                srcs = [(k3, v3)]
                if latent:
                    srcs = [(cache_diff_k[:, j].reshape(nb, past, DIFF_W),
                             cache_diff_v[:, j].reshape(nb, past, DIFF_W))] + srcs
                else:
                    ctx_out.setdefault('diff_k', []).append(k3.reshape(nb, s, DIFF_HEADS, 2 * DIFF_HD))
                    ctx_out.setdefault('diff_v', []).append(v3.reshape(nb, s, DIFF_HEADS, 2 * DIFF_HD))
                    ctx_out.setdefault('state', []).append(last)
                o = _diff_attn(q.reshape(nb, s, DIFF_W), e_lam[j], e_subln_g[j].reshape(1, LANES), srcs,
                               _diff_lambda_init(l))
                acts = [y_rnn.reshape(nb * s, D_RNN), o.reshape(nb * s, DIFF_W)]
                w_out = e_w_out[j].astype(BF16)
            else:
                outs = _odd_in(x, mrows, per_row, ng[1], o_w_in[j].astype(BF16),
                               _tile_gain(o_q_g[j], 2 * LANES), _tile_gain(o_k_g[j], 2 * LANES),
                               rope if latent else None)
                q, kd = outs[:2]
                v = outs[-1]
                n_kv = WIN_KV * WIN_HD
                ctx = None
                if latent:
                    ctx = (cache_win_k[:, j].reshape(nb, past, n_kv), cache_win_v[:, j].reshape(nb, past, n_kv))
                else:
                    ctx_out.setdefault('win_k', []).append(outs[2].reshape(nb, s, WIN_KV, WIN_HD))
                    ctx_out.setdefault('win_v', []).append(v.reshape(nb, s, WIN_KV, WIN_HD))
                o = _win_attn(q.reshape(nb, s, ODD_MIX), o_sink[j], kd.reshape(nb, s, 2 * n_kv),
                              v.reshape(nb, s, n_kv), ctx)
                acts = [o.reshape(nb * s, ODD_MIX)]
                w_out = o_w_out[j].astype(BF16)
            x = _ffn(x, mrows, per_row, ng[2], w1, w3, w2, l, 1, acts, w_out)
        finals.append(x.reshape(nb, s, D_MODEL))

    return (finals[0], finals[1],
            jnp.stack(ctx_out['diff_k'], axis=1), jnp.stack(ctx_out['diff_v'], axis=1),
            jnp.stack(ctx_out['state'], axis=1),
            jnp.stack(ctx_out['win_k'], axis=1), jnp.stack(ctx_out['win_v'], axis=1))
```

```python
import functools
import math

import numpy as np
import jax
import jax.numpy as jnp
from jax import lax
from jax.experimental import pallas as pl
from jax.experimental.pallas import tpu as pltpu

F32 = jnp.float32
BF16 = jnp.bfloat16

D_MODEL = 1024
DEPTH = 2
N_MOD = 9
D_FF = 2816
GRID_W = 64
ROPE_THETA = 10000.0
EPS = 1e-6
NEG_INF = -1e30

D_RNN = 512
RNN_BLOCKS = 8
RNN_BW = D_RNN // RNN_BLOCKS
LRU_C = 8.0

DIFF_HEADS = 4
DIFF_HD = 64
DIFF_W = DIFF_HEADS * 2 * DIFF_HD

WIN_HEADS = 16
WIN_KV = 4
WIN_G = WIN_HEADS // WIN_KV
WIN_HD = 64
WINDOW = 128

EVEN_IN = 2 * D_RNN + 3 * DIFF_W
ODD_IN = (WIN_HEADS + 2 * WIN_KV) * WIN_HD
ODD_MIX = WIN_HEADS * WIN_HD

LANES = 128
SUBLANES = 8
HEAD_HALF = LANES // 2
ROPE_PAIR = DIFF_HD // 4
ONES_ROWS = 16
LOG2E = math.log2(math.e)
Q_SCALE = DIFF_HD ** -0.5 * LOG2E

TOKEN_TILE = 512
FFN_TILE = 1024
IN_TILE = 1024
SUB_TILE = 256
FF_CHUNK = 256
MOD_COLS = 1152
COND_ROWS = 16
LRU_STEPS = 64
LRU_DOUBLE_BUFFER_MAX = 4 * 1024 * 1024
DIFF_Q_TILE = 512
DIFF_KEY_CHUNK = 512
DIFF_LOOKAHEAD = 2
WIN_Q_TILE = 128
WIN_TILES_PER_STEP = 8
WIN_LOOKAHEAD = 2
VMEM_LIMIT = 56 * 1024 * 1024


def _cparams(*sem):
    return pltpu.CompilerParams(dimension_semantics=sem, vmem_limit_bytes=VMEM_LIMIT)


def _resident(shape):
    return pl.BlockSpec(shape, lambda *_: (0,) * len(shape), pipeline_mode=pl.Buffered(1))


def _dot(a, b):
    return jnp.dot(a, b, preferred_element_type=F32)


def _dot_nt(a, b):
    return lax.dot_general(a, b, (((1,), (1,)), ((), ())), preferred_element_type=F32)


def _rms_mod(x, ng, shift, scale):
    ms = jnp.mean(x * x, axis=-1, keepdims=True)
    y = x * lax.rsqrt(ms + EPS) * ng
    return y * (1.0 + scale) + shift


def _silu(a):
    return a * jax.nn.sigmoid(a)


def _lo_mask():
    return lax.broadcasted_iota(jnp.int32, (1, LANES), 1) < HEAD_HALF


def _head_norm(t, g):
    lo = _lo_mask()
    sq = t * t
    s_lo = jnp.sum(jnp.where(lo, sq, 0.0), axis=-1, keepdims=True)
    s_hi = jnp.sum(jnp.where(lo, 0.0, sq), axis=-1, keepdims=True)
    inv = jnp.where(lo, lax.rsqrt(s_lo / DIFF_HD + EPS), lax.rsqrt(s_hi / DIFF_HD + EPS))
    return t * inv * g


def _head_norm_mxu(t, g, head_ones):
    sq = t * t
    hi = sq.astype(BF16)
    lo = (sq - hi.astype(F32)).astype(BF16)
    ss = _dot(hi, head_ones) + _dot(lo, head_ones)
    return t * lax.rsqrt(ss / DIFF_HD + EPS) * g


def _rope(t, cos, sin_signed):
    lane = lax.broadcasted_iota(jnp.int32, (1, LANES), 1)
    first = (lane % (2 * ROPE_PAIR)) < ROPE_PAIR
    partner = jnp.where(first, pltpu.roll(t, LANES - ROPE_PAIR, axis=1), pltpu.roll(t, ROPE_PAIR, axis=1))
    return t * cos + partner * sin_signed


def _swap_halves(t):
    return pltpu.roll(t, HEAD_HALF, axis=1)


def _dup_half(t, which):
    lo = _lo_mask()
    r = _swap_halves(t)
    return jnp.where(lo, t, r) if which == 0 else jnp.where(lo, r, t)


def _staggered(n_sub, prologue, matmuls, epilogue):
    h = prologue(0)
    pending = None
    for s in range(n_sub):
        d = matmuls(s, h)
        if s + 1 < n_sub:
            h = prologue(s + 1)
        if pending is not None:
            epilogue(*pending)
        pending = (s, d)
    epilogue(*pending)


def _mod_kernel(c_ref, w_ref, b_ref, o_ref):
    c = c_ref[...]
    s = _silu(c).astype(BF16)
    o_ref[...] = _dot(s, w_ref[...].astype(BF16)) + b_ref[...]


def _modulation(cond, w_mod, b_mod):
    n_col = N_MOD * D_MODEL
    return pl.pallas_call(
        _mod_kernel,
        grid=(DEPTH, n_col // MOD_COLS),
        in_specs=[
            pl.BlockSpec((COND_ROWS, D_MODEL), lambda l, j: (0, 0)),
            pl.BlockSpec((None, D_MODEL, MOD_COLS), lambda l, j: (l, 0, j)),
            pl.BlockSpec((None, 1, MOD_COLS), lambda l, j: (l, 0, j)),
        ],
        out_specs=pl.BlockSpec((None, COND_ROWS, MOD_COLS), lambda l, j: (l, 0, j)),
        out_shape=jax.ShapeDtypeStruct((DEPTH, COND_ROWS, n_col), F32),
        compiler_params=_cparams("arbitrary", "arbitrary"),
        name="modulation",
    )(cond, w_mod, b_mod.reshape(DEPTH, 1, n_col))


def _tok_spec(width, tm):
    return pl.BlockSpec((tm, width), lambda i: (i, 0))


def _mod_spec(tm, tokens_per_row):
    return pl.BlockSpec((None, N_MOD, D_MODEL), lambda i: ((i * tm) // tokens_per_row, 0, 0))


def _ffn_kernel(*refs, mi, n_act):
    x_ref, m_ref, ng_ref, w1_ref, w3_ref, w2_ref = refs[:6]
    acts = refs[6:6 + n_act]
    rest = refs[6 + n_act:]
    if n_act:
        wo_ref, o_ref, x_scr, g_scr = rest
    else:
        o_ref, x_scr, g_scr = rest

    def rows(s):
        return slice(s * TOKEN_TILE, (s + 1) * TOKEN_TILE)

    def prologue(s):
        x = x_ref[rows(s), :]
        if n_act:
            return x
        x_scr[s] = x
        return _rms_mod(x, ng_ref[...], m_ref[mi:mi + 1, :], m_ref[mi + 1:mi + 2, :]).astype(BF16)

    def matmuls(s, h):
        if n_act:
            y = None
            row = 0
            for a_ref in acts:
                ka = a_ref.shape[1]
                part = _dot(a_ref[rows(s), :], wo_ref[row:row + ka, :])
                y = part if y is None else y + part
                row += ka
            x = h + m_ref[5:6, :] * y
            x_scr[s] = x
            h = _rms_mod(x, ng_ref[...], m_ref[mi:mi + 1, :], m_ref[mi + 1:mi + 2, :]).astype(BF16)
        for j in range(D_FF // FF_CHUNK):
            cols = slice(j * FF_CHUNK, (j + 1) * FF_CHUNK)
            a = _dot(h, w1_ref[:, cols])
            b = _dot(h, w3_ref[:, cols])
            g_scr[:, cols] = (_silu(a) * b).astype(BF16)
        return _dot(g_scr[...], w2_ref[...])

    def epilogue(s, y):
        o_ref[rows(s), :] = x_scr[s] + 0.5 * m_ref[mi + 2:mi + 3, :] * y

    _staggered(x_ref.shape[0] // TOKEN_TILE, prologue, matmuls, epilogue)


def _ffn(x2d, mrows, tokens_per_row, ng, w1, w3, w2, layer, which, acts=(), w_out=None):
    n = x2d.shape[0]
    tm = FFN_TILE
    n_sub = tm // TOKEN_TILE
    mi = 6 * which

    def stack_spec(rows_, cols_):
        return pl.BlockSpec((None, None, rows_, cols_), lambda i: (layer, which, 0, 0), pipeline_mode=pl.Buffered(1))

    in_specs = [
        _tok_spec(D_MODEL, tm),
        _mod_spec(tm, tokens_per_row),
        _resident((1, D_MODEL)),
        stack_spec(D_MODEL, D_FF),
        stack_spec(D_MODEL, D_FF),
        stack_spec(D_FF, D_MODEL),
    ] + [_tok_spec(a.shape[1], tm) for a in acts]
    args = [x2d, mrows, ng, w1, w3, w2, *acts]
    if acts:
        in_specs.append(_resident(w_out.shape))
        args.append(w_out)
    return pl.pallas_call(
        functools.partial(_ffn_kernel, mi=mi, n_act=len(acts)),
        grid=(n // tm,),
        in_specs=in_specs,
        out_specs=_tok_spec(D_MODEL, tm),
        out_shape=jax.ShapeDtypeStruct((n, D_MODEL), F32),
        scratch_shapes=[pltpu.VMEM((n_sub, TOKEN_TILE, D_MODEL), F32), pltpu.VMEM((TOKEN_TILE, D_FF), BF16)],
        compiler_params=_cparams("arbitrary"),
        name="swiglu",
    )(*args)


def _even_in_kernel(*refs, latent):
    if latent:
        x_ref, m_ref, ng_ref, w_ref, qg_ref, kg_ref, cos_ref, sin_ref = refs[:8]
        outs = refs[8:]
    else:
        x_ref, m_ref, ng_ref, w_ref, qg_ref, kg_ref = refs[:6]
        outs = refs[6:]
    xr_ref, gt_ref, q_ref, k_ref, v_ref = outs
    base = 2 * D_RNN

    def rows(s):
        return slice(s * SUB_TILE, (s + 1) * SUB_TILE)

    def prologue(s):
        return _rms_mod(x_ref[rows(s), :], ng_ref[...], m_ref[3:4, :], m_ref[4:5, :]).astype(BF16)

    def matmuls(s, h):
        r = rows(s)
        xr_ref[r, :] = _dot(h, w_ref[:, 0:D_RNN])
        gt_ref[r, :] = _dot(h, w_ref[:, D_RNN:2 * D_RNN])
        v_ref[r, :] = _dot(h, w_ref[:, base + 2 * DIFF_W:base + 3 * DIFF_W]).astype(v_ref.dtype)
        return _dot(h, w_ref[:, base:base + DIFF_W]), _dot(h, w_ref[:, base + DIFF_W:base + 2 * DIFF_W])

    def epilogue(s, qk):
        q, k = qk
        r = rows(s)
        for hh in range(DIFF_HEADS):
            cols = slice(hh * LANES, (hh + 1) * LANES)
            qh = _head_norm(q[:, cols], qg_ref[...])
            kh = _head_norm(k[:, cols], kg_ref[...])
            if latent:
                qh = _rope(qh, cos_ref[r, :], sin_ref[r, :])
                kh = _rope(kh, cos_ref[r, :], sin_ref[r, :])
            q_ref[r, cols] = (qh * Q_SCALE).astype(BF16)
            k_ref[r, cols] = kh.astype(k_ref.dtype)

    _staggered(x_ref.shape[0] // SUB_TILE, prologue, matmuls, epilogue)


def _even_in(x2d, mrows, tokens_per_row, ng, w_in, qg, kg, rope):
    n = x2d.shape[0]
    tm = IN_TILE
    latent = rope is not None
    in_specs = [
        _tok_spec(D_MODEL, tm), _mod_spec(tm, tokens_per_row), _resident((1, D_MODEL)),
        _resident((D_MODEL, EVEN_IN)), _resident((1, LANES)), _resident((1, LANES)),
    ]
    args = [x2d, mrows, ng, w_in, qg, kg]
    if latent:
        seq = rope[0].shape[0]
        tab = pl.BlockSpec((tm, LANES), lambda i: (i % (seq // tm), 0))
        in_specs += [tab, tab]
        args += list(rope)
    kv_dtype = BF16 if latent else F32
    return pl.pallas_call(
        functools.partial(_even_in_kernel, latent=latent),
        grid=(n // tm,),
        in_specs=in_specs,
        out_specs=[_tok_spec(D_RNN, tm)] * 2 + [_tok_spec(DIFF_W, tm)] * 3,
        out_shape=[
            jax.ShapeDtypeStruct((n, D_RNN), F32), jax.ShapeDtypeStruct((n, D_RNN), F32),
            jax.ShapeDtypeStruct((n, DIFF_W), BF16),
            jax.ShapeDtypeStruct((n, DIFF_W), kv_dtype), jax.ShapeDtypeStruct((n, DIFF_W), kv_dtype),
        ],
        compiler_params=_cparams("arbitrary"),
        name="even_in",
    )(*args)


def _gelu_tanh(x):
    return 0.5 * x * (1.0 + jnp.tanh(math.sqrt(2.0 / math.pi) * (x + 0.044715 * (x * x * x))))


def _lru_kernel(xr_ref, gt_ref, cw_ref, cb_ref, wg_ref, bg_ref, lam_ref, h0_ref,
                y_ref, last_ref, hf_scr, hb_scr, xc_scr, af0, uf0, ab0, ub0, af1, uf1, ab1, ub1, *, seq, tt):
    n_chunks = seq // tt
    rows = tt * SUBLANES
    lam = lam_ref[...]
    m2sp = (0.25 * LRU_C) * (jnp.maximum(-lam, 0.0) + jnp.log1p(jnp.exp(-jnp.abs(lam))))

    def time_major(ref, t0, n):
        return jnp.swapaxes(ref[:, pl.ds(pl.multiple_of(t0, SUBLANES), n), :], 0, 1)

    def conv_chunk(c):
        t0 = c * tt
        before = time_major(xr_ref, jnp.maximum(t0 - SUBLANES, 0), SUBLANES)[SUBLANES - 2:]
        after = time_major(xr_ref, jnp.minimum(t0 + tt, seq - SUBLANES), SUBLANES)[:1]
        before = jnp.where(c > 0, before, 0.0)
        after = jnp.where(c < n_chunks - 1, after, 0.0)
        xw = jnp.concatenate([before, time_major(xr_ref, t0, tt), after], axis=0)
        xc = cb_ref[...]
        for tap in range(4):
            xc = xc + xw[tap:tap + tt] * cw_ref[tap:tap + 1, :]
        return xc.reshape(rows, LANES)

    def decay_and_input(th_r, th_i, xc, m2sp_row):
        nt = jnp.tanh(m2sp_row * th_r + m2sp_row)
        inv = 1.0 / (1.0 + nt)
        a = (1.0 - nt) * inv
        root = jnp.where(nt > 0.0, nt * lax.rsqrt(nt), 0.0)
        u = (root * inv) * ((th_i + 1.0) * xc)
        return a.reshape(tt, SUBLANES, LANES), u.reshape(tt, SUBLANES, LANES)

    def conv_pass(c, _):
        xc_scr[pl.ds(pl.multiple_of(c * rows, rows), rows), :] = conv_chunk(c)
        return 0

    lax.fori_loop(0, n_chunks, conv_pass, 0)

    def gates(c, d):
        xc = xc_scr[pl.ds(pl.multiple_of(c * rows, rows), rows), :]
        cols = slice(2 * d * LANES, 2 * (d + 1) * LANES)
        th = jnp.tanh(_dot(xc.astype(BF16), wg_ref[:, cols]) + bg_ref[:, cols])
        return decay_and_input(th[:, :LANES], th[:, LANES:], xc, m2sp[d:d + 1, :])

    bufs = ((af0, uf0, ab0, ub0), (af1, uf1, ab1, ub1))

    def fill(c, buf):
        buf[0][...], buf[1][...] = gates(c, 0)
        buf[2][...], buf[3][...] = gates(n_chunks - 1 - c, 1)

    def scan(c, buf, carry):
        af, uf, ab, ub = buf
        hf, hb = carry
        f0 = c * tt
        b0 = (n_chunks - 1 - c) * tt
        for i in range(tt):
            hf = af[i] * hf + uf[i]
            hf_scr[f0 + i] = hf
            ib = tt - 1 - i
            hb = ab[ib] * hb + ub[ib]
            hb_scr[b0 + ib] = hb
        return hf, hb

    def two_trips(k, carry):
        c = 2 * k
        fill(c + 1, bufs[1])
        carry = scan(c, bufs[0], carry)
        fill(jnp.minimum(c + 2, n_chunks - 1), bufs[0])
        return scan(c + 1, bufs[1], carry)

    fill(0, bufs[0])
    hf, hb = lax.fori_loop(0, n_chunks // 2, two_trips, (h0_ref[:, 0, :], h0_ref[:, 1, :]))
    last_ref[:, 0, :] = hf
    last_ref[:, 1, :] = hb

    def finish(c, _):
        t0 = pl.multiple_of(c * tt, tt)
        h = jnp.swapaxes(hf_scr[pl.ds(t0, tt)] + hb_scr[pl.ds(t0, tt)], 0, 1)
        y_ref[:, pl.ds(t0, tt), :] = (h * _gelu_tanh(gt_ref[:, pl.ds(t0, tt), :])).astype(y_ref.dtype)
        return 0

    lax.fori_loop(0, n_chunks, finish, 0)


def _lru(xr, gt, conv_w, conv_b, wg, bg, lam, h0):
    b, seq, _ = xr.shape
    tt = min(LRU_STEPS, seq)
    blk_bytes = seq * SUBLANES * LANES * 4
    mode = dict(pipeline_mode=pl.Buffered(1)) if blk_bytes > LRU_DOUBLE_BUFFER_MAX else {}
    seq_in = pl.BlockSpec((SUBLANES, seq, LANES), lambda i, j: (i, 0, j), **mode)
    seq_out = pl.BlockSpec((SUBLANES, seq, LANES), lambda i, j: (i, 0, j))
    st_spec = pl.BlockSpec((SUBLANES, 2, LANES), lambda i, j: (i, 0, j))
    return pl.pallas_call(
        functools.partial(_lru_kernel, seq=seq, tt=tt),
        grid=(b // SUBLANES, D_RNN // LANES),
        in_specs=[seq_in, seq_in,
                  pl.BlockSpec((4, LANES), lambda i, j: (0, j)), pl.BlockSpec((1, LANES), lambda i, j: (0, j)),
                  pl.BlockSpec((None, LANES, 4 * LANES), lambda i, j: (j, 0, 0)),
                  pl.BlockSpec((None, 1, 4 * LANES), lambda i, j: (j, 0, 0)),
                  pl.BlockSpec((2, LANES), lambda i, j: (0, j)), st_spec],
        out_specs=[seq_out, st_spec],
        out_shape=[jax.ShapeDtypeStruct((b, seq, D_RNN), BF16), jax.ShapeDtypeStruct((b, 2, D_RNN), F32)],
        scratch_shapes=[pltpu.VMEM((seq, SUBLANES, LANES), F32)] * 2 + [pltpu.VMEM((seq * SUBLANES, LANES), F32)]
        + [pltpu.VMEM((tt, SUBLANES, LANES), F32)] * 8,
        compiler_params=_cparams("arbitrary", "arbitrary"),
        name="rglru",
    )(xr, gt, conv_w, conv_b, wg, bg, lam, h0)


def _diff_attn_kernel(*refs, n_src, lam_init):
    q_ref, lamv_ref, sg_ref = refs[:3]
    srcs = [(refs[3 + 2 * s], refs[4 + 2 * s]) for s in range(n_src)]
    o_ref = refs[3 + 2 * n_src]
    scr = refs[4 + 2 * n_src:]
    kb = [scr[2 * s] for s in range(n_src)]
    vt = [scr[2 * s + 1] for s in range(n_src)]

    @pl.when(pl.program_id(1) == 0)
    def _():
        for s, (k_ref, v_ref) in enumerate(srcs):
            kb[s][...] = k_ref[...].astype(BF16)
            for hh in range(DIFF_HEADS):
                cols = slice(hh * LANES, (hh + 1) * LANES)
                vt[s][hh, 0:LANES, :] = v_ref[:, cols].astype(F32).T.astype(BF16)
                vt[s][hh, LANES:, :] = jnp.ones((ONES_ROWS, v_ref.shape[0]), BF16)

    lv = lamv_ref[...]
    lam = (jnp.exp(jnp.sum(lv[0:1, :] * lv[1:2, :], axis=-1, keepdims=True))
           - jnp.exp(jnp.sum(lv[2:3, :] * lv[3:4, :], axis=-1, keepdims=True)) + lam_init)
    lo = _lo_mask()
    zero = jnp.zeros((), BF16)
    tq = q_ref.shape[0]

    chunks = []
    for s, (k_ref, _) in enumerate(srcs):
        t_s = k_ref.shape[0]
        step = min(DIFF_KEY_CHUNK, t_s)
        chunks += [(s, t0, step) for t0 in range(0, t_s, step)]
    stages = [(hh, ci) for hh in range(DIFF_HEADS) for ci in range(len(chunks))]

    def masked_queries(hh):
        qh = q_ref[:, hh * LANES:(hh + 1) * LANES]
        return jnp.concatenate([jnp.where(lo, qh, zero), jnp.where(lo, zero, qh)], axis=0)

    qs = [masked_queries(hh) for hh in range(DIFF_HEADS)]

    def scores(hh, ci):
        s, t0, n = chunks[ci]
        return _dot_nt(kb[s][t0:t0 + n, hh * LANES:(hh + 1) * LANES], qs[hh])

    queue = [scores(*stages[i]) for i in range(min(DIFF_LOOKAHEAD, len(stages)))]
    m = acc = None
    for idx, (hh, ci) in enumerate(stages):
        st = queue.pop(0)
        if idx + DIFF_LOOKAHEAD < len(stages):
            queue.append(scores(*stages[idx + DIFF_LOOKAHEAD]))
        s, t0, n = chunks[ci]
        vt_c = vt[s][hh, :, t0:t0 + n]
        m_c = jnp.max(st, axis=0, keepdims=True)
        if ci == 0:
            m = m_c
            acc = _dot(vt_c, jnp.exp2(st - m).astype(BF16))
        else:
            m_new = jnp.maximum(m, m_c)
            acc = acc * jnp.exp2(m - m_new) + _dot(vt_c, jnp.exp2(st - m_new).astype(BF16))
            m = m_new
        if ci == len(chunks) - 1:
            cols = slice(hh * LANES, (hh + 1) * LANES)
            o2t = acc[0:LANES, :] * (1.0 / acc[LANES:LANES + 1, :])
            o = (o2t[:, :tq] - lam * o2t[:, tq:]).T
            ms = jnp.mean(o * o, axis=-1, keepdims=True)
            o = o * lax.rsqrt(ms + EPS) * sg_ref[...] * (1.0 - lam_init)
            o_ref[:, cols] = o.astype(o_ref.dtype)


def _diff_attn(q, lam_vec, subln_g, srcs, lam_init):
    b, seq, _ = q.shape
    tq = min(DIFF_Q_TILE, seq)
    q_spec = pl.BlockSpec((None, tq, DIFF_W), lambda i, j: (i, j, 0))
    in_specs = [q_spec, _resident((4, DIFF_HD)), _resident((1, LANES))]
    args = [q, lam_vec, subln_g]
    scratch = []
    for k, v in srcs:
        kv_spec = pl.BlockSpec((None, k.shape[1], DIFF_W), lambda i, j: (i, 0, 0))
        in_specs += [kv_spec, kv_spec]
        args += [k, v]
        scratch += [pltpu.VMEM((k.shape[1], DIFF_W), BF16),
                    pltpu.VMEM((DIFF_HEADS, LANES + ONES_ROWS, k.shape[1]), BF16)]
    return pl.pallas_call(
        functools.partial(_diff_attn_kernel, n_src=len(srcs), lam_init=lam_init),
        grid=(b, seq // tq),
        in_specs=in_specs,
        out_specs=q_spec,
        out_shape=jax.ShapeDtypeStruct((b, seq, DIFF_W), BF16),
        scratch_shapes=scratch,
        compiler_params=_cparams("arbitrary", "arbitrary"),
        name="diff_attn",
    )(*args)


def _odd_in_kernel(*refs, latent):
    if latent:
        x_ref, m_ref, ng_ref, w_ref, qg_ref, kg_ref, ones_ref, cos_ref, sin_ref = refs[:9]
        q_ref, kd_ref, v_ref = refs[9:]
    else:
        x_ref, m_ref, ng_ref, w_ref, qg_ref, kg_ref, ones_ref = refs[:7]
        q_ref, kd_ref, k_ref, v_ref = refs[7:]
    n_q = WIN_HEADS * WIN_HD
    n_kv = WIN_KV * WIN_HD

    def rows(s):
        return slice(s * SUB_TILE, (s + 1) * SUB_TILE)

    def prologue(s):
        return _rms_mod(x_ref[rows(s), :], ng_ref[...], m_ref[3:4, :], m_ref[4:5, :]).astype(BF16)

    def matmuls(s, h):
        v_ref[rows(s), :] = _dot(h, w_ref[:, n_q + n_kv:n_q + 2 * n_kv]).astype(v_ref.dtype)
        return _dot(h, w_ref[:, 0:n_q]), _dot(h, w_ref[:, n_q:n_q + n_kv])

    def epilogue(s, qk):
        q, k = qk
        r = rows(s)
        for pair in range(n_q // (2 * LANES)):
            qn = _head_norm_mxu(q[:, 2 * pair * LANES:2 * (pair + 1) * LANES], qg_ref[...], ones_ref[...])
            for half in range(2):
                cols = slice((2 * pair + half) * LANES, (2 * pair + half + 1) * LANES)
                qh = qn[:, half * LANES:(half + 1) * LANES]
                if latent:
                    qh = _rope(qh, cos_ref[r, :], sin_ref[r, :])
                q_ref[r, cols] = (qh * Q_SCALE).astype(BF16)
        kn = _head_norm_mxu(k, kg_ref[...], ones_ref[...])
        for blk in range(n_kv // LANES):
            cols = slice(blk * LANES, (blk + 1) * LANES)
            kh = kn[:, cols]
            if latent:
                kh = _rope(kh, cos_ref[r, :], sin_ref[r, :])
            else:
                k_ref[r, cols] = kh
            for half in range(2):
                dst = slice((2 * blk + half) * LANES, (2 * blk + half + 1) * LANES)
                kd_ref[r, dst] = _dup_half(kh, half).astype(BF16)

    _staggered(x_ref.shape[0] // SUB_TILE, prologue, matmuls, epilogue)


def _odd_in(x2d, mrows, tokens_per_row, ng, w_in, qg, kg, rope):
    n = x2d.shape[0]
    tm = IN_TILE
    latent = rope is not None
    n_q = WIN_HEADS * WIN_HD
    n_kv = WIN_KV * WIN_HD
    in_specs = [
        _tok_spec(D_MODEL, tm), _mod_spec(tm, tokens_per_row), _resident((1, D_MODEL)),
        _resident((D_MODEL, ODD_IN)), _resident((1, 2 * LANES)), _resident((1, 2 * LANES)),
        _resident((2 * LANES, 2 * LANES)),
    ]
    args = [x2d, mrows, ng, w_in, qg, kg, _head_ones()]
    out_specs = [_tok_spec(n_q, tm), _tok_spec(2 * n_kv, tm)]
    out_shape = [jax.ShapeDtypeStruct((n, n_q), BF16), jax.ShapeDtypeStruct((n, 2 * n_kv), BF16)]
    if latent:
        seq = rope[0].shape[0]
        tab = pl.BlockSpec((tm, LANES), lambda i: (i % (seq // tm), 0))
        in_specs += [tab, tab]
        args += list(rope)
        out_specs += [_tok_spec(n_kv, tm)]
        out_shape += [jax.ShapeDtypeStruct((n, n_kv), BF16)]
    else:
        out_specs += [_tok_spec(n_kv, tm)] * 2
        out_shape += [jax.ShapeDtypeStruct((n, n_kv), F32)] * 2
    return pl.pallas_call(
        functools.partial(_odd_in_kernel, latent=latent),
        grid=(n // tm,),
        in_specs=in_specs,
        out_specs=out_specs,
        out_shape=out_shape,
        compiler_params=_cparams("arbitrary"),
        name="odd_in",
    )(*args)


def _win_attn_kernel(*refs, latent, seq, tq):
    if latent:
        q_ref, sink_ref, kd_ref, v_ref, ck_ref, cv_ref, o_ref, vt_scr, ckd_scr, cvt_scr = refs
    else:
        q_ref, sink_ref, kd_ref, v_ref, o_ref, vt_scr = refs
    lo = _lo_mask()
    zero = jnp.zeros((), BF16)
    n_cols = WIN_G * tq
    n_kv = WIN_KV * WIN_HD

    @pl.when(pl.program_id(1) == 0)
    def _():
        for blk in range(n_kv // LANES):
            vt2 = v_ref[:, blk * LANES:(blk + 1) * LANES].astype(F32).T.astype(BF16)
            for half in range(2):
                vt_scr[2 * blk + half, 0:WIN_HD, :] = vt2[half * WIN_HD:(half + 1) * WIN_HD, :]
        for j in range(WIN_KV):
            vt_scr[j, WIN_HD:, :] = jnp.ones((ONES_ROWS, v_ref.shape[0]), BF16)
        if latent:
            for j in range(WIN_KV):
                kj = ck_ref[:, j, :]
                vj = cv_ref[:, j, :]
                ckd_scr[:, j * LANES:(j + 1) * LANES] = jnp.concatenate([kj, kj], axis=1).astype(BF16)
                cvt_scr[j, 0:WIN_HD, :] = jnp.concatenate([vj, vj], axis=1).T[0:WIN_HD, :].astype(BF16)
                cvt_scr[j, WIN_HD:, :] = jnp.ones((ONES_ROWS, cv_ref.shape[0]), BF16)

    n_tiles = q_ref.shape[0] // tq
    starts, biases = [], []
    if latent:
        span = 3 * tq
        for t in range(n_tiles):
            i = pl.program_id(1) * n_tiles + t
            start = pl.multiple_of(jnp.clip((i - 1) * tq, 0, seq - span), tq)
            kpos = start + lax.broadcasted_iota(jnp.int32, (span, 1), 0)
            qpos = i * tq + (lax.broadcasted_iota(jnp.int32, (1, n_cols), 1) & (tq - 1))
            starts.append(start)
            biases.append(jnp.where(jnp.abs(kpos - qpos) <= WINDOW, 0.0, NEG_INF))

    def scores(t, j):
        kcols = slice(j * LANES, (j + 1) * LANES)
        qs = []
        for pair in range(WIN_G // 2):
            qb = q_ref[t * tq:(t + 1) * tq, (2 * j + pair) * LANES:(2 * j + pair + 1) * LANES]
            qs += [jnp.where(lo, qb, zero), jnp.where(lo, zero, qb)]
        qs = jnp.concatenate(qs, axis=0)
        if latent:
            return [_dot_nt(kd_ref[pl.ds(starts[t], span), kcols], qs) + biases[t], _dot_nt(ckd_scr[:, kcols], qs)]
        return [_dot_nt(kd_ref[:, kcols], qs)]

    stages = [(t, j) for t in range(n_tiles) for j in range(WIN_KV)]
    queue = [scores(*stages[n]) for n in range(min(WIN_LOOKAHEAD, len(stages)))]
    for n, (t, j) in enumerate(stages):
        sts = queue.pop(0)
        if n + WIN_LOOKAHEAD < len(stages):
            queue.append(scores(*stages[n + WIN_LOOKAHEAD]))
        trows = slice(t * tq, (t + 1) * tq)
        sink = jnp.concatenate([jnp.full((1, tq), sink_ref[WIN_G * j + g] * LOG2E, F32) for g in range(WIN_G)],
                               axis=1)
        vts = [vt_scr[j, :, pl.ds(starts[t], span)], cvt_scr[j]] if latent else [vt_scr[j]]
        m = functools.reduce(jnp.maximum, [jnp.max(st, axis=0, keepdims=True) for st in sts])
        m = jnp.maximum(m, sink)
        ps = [jnp.exp2(st - m) for st in sts]
        ot = functools.reduce(jnp.add, [_dot(vt, p.astype(BF16)) for vt, p in zip(vts, ps)])
        den = ot[WIN_HD:WIN_HD + 1, :] + jnp.exp2(sink - m)
        ot = ot[0:WIN_HD, :] * (1.0 / den)
        for pair in range(WIN_G // 2):
            blk = 2 * j + pair
            c0 = 2 * pair * tq
            both = jnp.concatenate([ot[:, c0:c0 + tq], ot[:, c0 + tq:c0 + 2 * tq]], axis=0)
            o_ref[trows, blk * LANES:(blk + 1) * LANES] = both.T.astype(o_ref.dtype)


def _win_attn(q, sink, kd, v, ctx):
    b, seq, n_q = q.shape
    n_kv = v.shape[2]
    latent = ctx is not None
    tq = WIN_Q_TILE if latent else min(seq, 2 * WIN_Q_TILE)
    step_rows = WIN_TILES_PER_STEP * tq if latent else tq
    q_spec = pl.BlockSpec((None, step_rows, n_q), lambda i, j: (i, j, 0))
    kd_spec = pl.BlockSpec((None, seq, kd.shape[2]), lambda i, j: (i, 0, 0))
    v_spec = pl.BlockSpec((None, seq, n_kv), lambda i, j: (i, 0, 0))
    in_specs = [q_spec, pl.BlockSpec(memory_space=pltpu.SMEM), kd_spec, v_spec]
    args = [q, sink, kd, v]
    scratch = [pltpu.VMEM((WIN_KV, WIN_HD + ONES_ROWS, seq), BF16)]
    if latent:
        ck, cv, layer = ctx
        past = ck.shape[2]
        c_spec = pl.BlockSpec((None, None, past, WIN_KV, WIN_HD), lambda i, j: (i, layer, 0, 0, 0))
        in_specs += [c_spec, c_spec]
        args += [ck, cv]
        scratch += [pltpu.VMEM((past, kd.shape[2]), BF16), pltpu.VMEM((WIN_KV, WIN_HD + ONES_ROWS, past), BF16)]
    return pl.pallas_call(
        functools.partial(_win_attn_kernel, latent=latent, seq=seq, tq=tq),
        grid=(b, seq // step_rows),
        in_specs=in_specs,
        out_specs=q_spec,
        out_shape=jax.ShapeDtypeStruct((b, seq, n_q), BF16),
        scratch_shapes=scratch,
        compiler_params=_cparams("arbitrary", "arbitrary"),
        name="win_attn",
    )(*args)


def _rope_tables(seq):
    rows = seq // GRID_W
    row = np.repeat(np.arange(rows), GRID_W).astype(np.float32)
    col = np.tile(np.arange(GRID_W), rows).astype(np.float32)
    half = DIFF_HD // 2
    inv = (ROPE_THETA ** (-np.arange(0, half, 2, dtype=np.float32) / half)).astype(np.float32)
    ang_r = row[:, None] * inv[None, :]
    ang_c = col[:, None] * inv[None, :]
    cos = np.concatenate([np.cos(ang_r), np.cos(ang_r), np.cos(ang_c), np.cos(ang_c)], axis=1)
    sin = np.concatenate([-np.sin(ang_r), np.sin(ang_r), -np.sin(ang_c), np.sin(ang_c)], axis=1)
    cos = np.tile(cos, (1, LANES // DIFF_HD)).astype(np.float32)
    sin = np.tile(sin, (1, LANES // DIFF_HD)).astype(np.float32)
    return jnp.asarray(cos), jnp.asarray(sin)


def _block_diag(w):
    eye = jnp.eye(RNN_BLOCKS, dtype=w.dtype)
    return jnp.einsum('nkj,nm->nkmj', w, eye).reshape(D_RNN, D_RNN)


def _gate_params(wa, ba, wi, bi):
    mats = [_block_diag(wa[0]), _block_diag(wi[0]), _block_diag(wa[1]), _block_diag(wi[1])]
    vecs = [ba[0], bi[0], ba[1], bi[1]]
    w_blocks, b_blocks = [], []
    for blk in range(D_RNN // LANES):
        sl = slice(blk * LANES, (blk + 1) * LANES)
        w_blocks.append(jnp.concatenate([m[sl, sl] for m in mats], axis=1))
        b_blocks.append(jnp.concatenate([v[sl] for v in vecs]).reshape(1, -1))
    return (0.5 * jnp.stack(w_blocks)).astype(BF16), 0.5 * jnp.stack(b_blocks)


def _tile_gain(g, width=LANES):
    return jnp.tile(g, width // g.shape[0]).reshape(1, width)


def _head_ones():
    head = np.arange(2 * LANES) // WIN_HD
    return jnp.asarray((head[:, None] == head[None, :]).astype(np.float32), dtype=BF16)


def _diff_lambda_init(layer):
    return 0.8 - 0.6 * math.exp(-0.3 * layer)


def kernel(x_prompt, x_sample, cache_diff_k, cache_diff_v, state_lru, cache_win_k, cache_win_v, c, c_ctx,
           norm_g, w_mod, b_mod, ffn_w1, ffn_w3, ffn_w2, e_w_in, e_w_out, e_conv_w, e_conv_b,
           e_lru_wa, e_lru_ba, e_lru_wi, e_lru_bi, e_lru_lam, e_q_g, e_k_g, e_lam, e_subln_g,
           o_w_in, o_w_out, o_q_g, o_k_g, o_sink):
    batch, seq, _ = x_prompt.shape
    dec_batch, dec_seq, _ = x_sample.shape
    past = cache_diff_k.shape[2]

    cond = jnp.concatenate([c_ctx[None, :], c], axis=0)
    cond = jnp.pad(cond, ((0, COND_ROWS - cond.shape[0]), (0, 0)))
    mod = _modulation(cond, w_mod, b_mod).reshape(DEPTH, COND_ROWS, N_MOD, D_MODEL)

    w1 = ffn_w1.astype(BF16)
    w3 = ffn_w3.astype(BF16)
    w2 = ffn_w2.astype(BF16)
    rope = _rope_tables(dec_seq)

    groups = [
        dict(x=x_prompt.reshape(batch * seq, D_MODEL), b=batch, s=seq, latent=False,
             rows=slice(0, 1), per_row=batch * seq),
        dict(x=x_sample.reshape(dec_batch * dec_seq, D_MODEL), b=dec_batch, s=dec_seq, latent=True,
             rows=slice(1, 1 + dec_batch), per_row=dec_seq),
    ]
    ctx_out = {}
    finals = []
    for grp in groups:
        x = grp['x']
        nb, s, latent, per_row = grp['b'], grp['s'], grp['latent'], grp['per_row']
        for l in range(DEPTH):
            j = l // 2
            mrows = mod[l, grp['rows']]
            ng = norm_g[l].reshape(3, 1, D_MODEL)
            x = _ffn(x, mrows, per_row, ng[0], w1, w3, w2, l, 0)
            if l % 2 == 0:
                xr, gt, q, k, v = _even_in(x, mrows, per_row, ng[1], e_w_in[j].astype(BF16),
                                           _tile_gain(e_q_g[j]), _tile_gain(e_k_g[j]), rope if latent else None)
                wg, bg = _gate_params(e_lru_wa[j], e_lru_ba[j], e_lru_wi[j], e_lru_bi[j])
                h0 = state_lru[:, j] if latent else jnp.zeros((nb, 2, D_RNN), F32)
                y_rnn, last = _lru(xr.reshape(nb, s, D_RNN), gt.reshape(nb, s, D_RNN), e_conv_w[j],
                                   e_conv_b[j].reshape(1, D_RNN), wg, bg, e_lru_lam[j], h0)
                k3 = k.reshape(nb, s, DIFF_W)
                v3 = v.reshape(nb, s, DIFF_W)
                srcs = [(k3, v3)]
                if latent:
                    srcs = [(cache_diff_k[:, j].reshape(nb, past, DIFF_W),
                             cache_diff_v[:, j].reshape(nb, past, DIFF_W))] + srcs
                else:
                    ctx_out.setdefault('diff_k', []).append(k3.reshape(nb, s, DIFF_HEADS, 2 * DIFF_HD))
                    ctx_out.setdefault('diff_v', []).append(v3.reshape(nb, s, DIFF_HEADS, 2 * DIFF_HD))
                    ctx_out.setdefault('state', []).append(last)
                o = _diff_attn(q.reshape(nb, s, DIFF_W), e_lam[j], e_subln_g[j].reshape(1, LANES), srcs,
                               _diff_lambda_init(l))
                acts = [y_rnn.reshape(nb * s, D_RNN), o.reshape(nb * s, DIFF_W)]
                w_out = e_w_out[j].astype(BF16)
            else:
                outs = _odd_in(x, mrows, per_row, ng[1], o_w_in[j].astype(BF16),
                               _tile_gain(o_q_g[j], 2 * LANES), _tile_gain(o_k_g[j], 2 * LANES),
                               rope if latent else None)
                q, kd = outs[:2]
                v = outs[-1]
                n_kv = WIN_KV * WIN_HD
                ctx = None
                if latent:
                    ctx = (cache_win_k, cache_win_v, j)
                else:
                    ctx_out.setdefault('win_k', []).append(outs[2].reshape(nb, s, WIN_KV, WIN_HD))
                    ctx_out.setdefault('win_v', []).append(v.reshape(nb, s, WIN_KV, WIN_HD))
                o = _win_attn(q.reshape(nb, s, ODD_MIX), o_sink[j], kd.reshape(nb, s, 2 * n_kv),
                              v.reshape(nb, s, n_kv), ctx)
                acts = [o.reshape(nb * s, ODD_MIX)]
                w_out = o_w_out[j].astype(BF16)
            x = _ffn(x, mrows, per_row, ng[2], w1, w3, w2, l, 1, acts, w_out)
        finals.append(x.reshape(nb, s, D_MODEL))

    return (finals[0], finals[1],
            jnp.stack(ctx_out['diff_k'], axis=1), jnp.stack(ctx_out['diff_v'], axis=1),
            jnp.stack(ctx_out['state'], axis=1),
            jnp.stack(ctx_out['win_k'], axis=1), jnp.stack(ctx_out['win_v'], axis=1))
```

```python
import functools
import math

import numpy as np
import jax
import jax.numpy as jnp
from jax import lax
from jax.experimental import pallas as pl
from jax.experimental.pallas import tpu as pltpu

F32 = jnp.float32
BF16 = jnp.bfloat16

D_MODEL = 1024
DEPTH = 2
N_MOD = 9
D_FF = 2816
GRID_W = 64
ROPE_THETA = 10000.0
EPS = 1e-6
NEG_INF = -1e30

D_RNN = 512
RNN_BLOCKS = 8
RNN_BW = D_RNN // RNN_BLOCKS
LRU_C = 8.0

DIFF_HEADS = 4
DIFF_HD = 64
DIFF_W = DIFF_HEADS * 2 * DIFF_HD

WIN_HEADS = 16
WIN_KV = 4
WIN_G = WIN_HEADS // WIN_KV
WIN_HD = 64
WINDOW = 128

EVEN_IN = 2 * D_RNN + 3 * DIFF_W
ODD_IN = (WIN_HEADS + 2 * WIN_KV) * WIN_HD
ODD_MIX = WIN_HEADS * WIN_HD

LANES = 128
SUBLANES = 8
HEAD_HALF = LANES // 2
ROPE_PAIR = DIFF_HD // 4
ONES_ROWS = 16
LOG2E = math.log2(math.e)
Q_SCALE = DIFF_HD ** -0.5 * LOG2E

TOKEN_TILE = 512
FFN_TILE = 1024
IN_TILE = 1024
SUB_TILE = 256
FF_CHUNK = 256
MOD_COLS = 1152
COND_ROWS = 16
LRU_STEPS = 64
LRU_DOUBLE_BUFFER_MAX = 4 * 1024 * 1024
DIFF_Q_TILE = 512
DIFF_KEY_CHUNK = 512
DIFF_LOOKAHEAD = 2
ATTN_STEP_ROWS = 1024
WIN_Q_TILE = 128
WIN_TILES_PER_STEP = 8
WIN_LOOKAHEAD = 2
VMEM_LIMIT = 56 * 1024 * 1024


def _cparams(*sem):
    return pltpu.CompilerParams(dimension_semantics=sem, vmem_limit_bytes=VMEM_LIMIT)


def _resident(shape):
    return pl.BlockSpec(shape, lambda *_: (0,) * len(shape), pipeline_mode=pl.Buffered(1))


def _dot(a, b):
    return jnp.dot(a, b, preferred_element_type=F32)


def _dot_nt(a, b):
    return lax.dot_general(a, b, (((1,), (1,)), ((), ())), preferred_element_type=F32)


def _rms_mod(x, ng, shift, scale):
    ms = jnp.mean(x * x, axis=-1, keepdims=True)
    y = x * lax.rsqrt(ms + EPS) * ng
    return y * (1.0 + scale) + shift


def _silu(a):
    return a * jax.nn.sigmoid(a)


def _lo_mask():
    return lax.broadcasted_iota(jnp.int32, (1, LANES), 1) < HEAD_HALF


def _head_norm(t, g):
    lo = _lo_mask()
    sq = t * t
    s_lo = jnp.sum(jnp.where(lo, sq, 0.0), axis=-1, keepdims=True)
    s_hi = jnp.sum(jnp.where(lo, 0.0, sq), axis=-1, keepdims=True)
    inv = jnp.where(lo, lax.rsqrt(s_lo / DIFF_HD + EPS), lax.rsqrt(s_hi / DIFF_HD + EPS))
    return t * inv * g


def _head_norm_mxu(t, g, head_ones):
    sq = t * t
    hi = sq.astype(BF16)
    lo = (sq - hi.astype(F32)).astype(BF16)
    ss = _dot(hi, head_ones) + _dot(lo, head_ones)
    return t * lax.rsqrt(ss / DIFF_HD + EPS) * g


def _rope(t, cos, sin_signed):
    lane = lax.broadcasted_iota(jnp.int32, (1, LANES), 1)
    first = (lane % (2 * ROPE_PAIR)) < ROPE_PAIR
    partner = jnp.where(first, pltpu.roll(t, LANES - ROPE_PAIR, axis=1), pltpu.roll(t, ROPE_PAIR, axis=1))
    return t * cos + partner * sin_signed


def _swap_halves(t):
    return pltpu.roll(t, HEAD_HALF, axis=1)


def _dup_half(t, which):
    lo = _lo_mask()
    r = _swap_halves(t)
    return jnp.where(lo, t, r) if which == 0 else jnp.where(lo, r, t)


def _batch_group(batch, seq):
    return max(1, min(batch, ATTN_STEP_ROWS // seq))


def _staggered(n_sub, prologue, matmuls, epilogue):
    h = prologue(0)
    pending = None
    for s in range(n_sub):
        d = matmuls(s, h)
        if s + 1 < n_sub:
            h = prologue(s + 1)
        if pending is not None:
            epilogue(*pending)
        pending = (s, d)
    epilogue(*pending)


def _mod_kernel(c_ref, w_ref, b_ref, o_ref):
    c = c_ref[...]
    s = _silu(c).astype(BF16)
    o_ref[...] = _dot(s, w_ref[...].astype(BF16)) + b_ref[...]


def _modulation(cond, w_mod, b_mod):
    n_col = N_MOD * D_MODEL
    return pl.pallas_call(
        _mod_kernel,
        grid=(DEPTH, n_col // MOD_COLS),
        in_specs=[
            pl.BlockSpec((COND_ROWS, D_MODEL), lambda l, j: (0, 0)),
            pl.BlockSpec((None, D_MODEL, MOD_COLS), lambda l, j: (l, 0, j)),
            pl.BlockSpec((None, 1, MOD_COLS), lambda l, j: (l, 0, j)),
        ],
        out_specs=pl.BlockSpec((None, COND_ROWS, MOD_COLS), lambda l, j: (l, 0, j)),
        out_shape=jax.ShapeDtypeStruct((DEPTH, COND_ROWS, n_col), F32),
        compiler_params=_cparams("arbitrary", "arbitrary"),
        name="modulation",
    )(cond, w_mod, b_mod.reshape(DEPTH, 1, n_col))


def _tok_spec(width, tm):
    return pl.BlockSpec((tm, width), lambda i: (i, 0))


def _mod_spec(tm, tokens_per_row):
    return pl.BlockSpec((None, N_MOD, D_MODEL), lambda i: ((i * tm) // tokens_per_row, 0, 0))


def _ffn_kernel(*refs, mi, n_act):
    x_ref, m_ref, ng_ref, w1_ref, w3_ref, w2_ref = refs[:6]
    acts = refs[6:6 + n_act]
    rest = refs[6 + n_act:]
    if n_act:
        wo_ref, o_ref, x_scr, g_scr = rest
    else:
        o_ref, x_scr, g_scr = rest

    def rows(s):
        return slice(s * TOKEN_TILE, (s + 1) * TOKEN_TILE)

    def prologue(s):
        x = x_ref[rows(s), :]
        if n_act:
            return x
        x_scr[s] = x
        return _rms_mod(x, ng_ref[...], m_ref[mi:mi + 1, :], m_ref[mi + 1:mi + 2, :]).astype(BF16)

    def matmuls(s, h):
        if n_act:
            y = None
            row = 0
            for a_ref in acts:
                ka = a_ref.shape[1]
                part = _dot(a_ref[rows(s), :], wo_ref[row:row + ka, :])
                y = part if y is None else y + part
                row += ka
            x = h + m_ref[5:6, :] * y
            x_scr[s] = x
            h = _rms_mod(x, ng_ref[...], m_ref[mi:mi + 1, :], m_ref[mi + 1:mi + 2, :]).astype(BF16)
        for j in range(D_FF // FF_CHUNK):
            cols = slice(j * FF_CHUNK, (j + 1) * FF_CHUNK)
            a = _dot(h, w1_ref[:, cols])
            b = _dot(h, w3_ref[:, cols])
            g_scr[:, cols] = (_silu(a) * b).astype(BF16)
        return _dot(g_scr[...], w2_ref[...])

    def epilogue(s, y):
        o_ref[rows(s), :] = x_scr[s] + 0.5 * m_ref[mi + 2:mi + 3, :] * y

    _staggered(x_ref.shape[0] // TOKEN_TILE, prologue, matmuls, epilogue)


def _ffn(x2d, mrows, tokens_per_row, ng, w1, w3, w2, layer, which, acts=(), w_out=None):
    n = x2d.shape[0]
    tm = FFN_TILE
    n_sub = tm // TOKEN_TILE
    mi = 6 * which

    def stack_spec(rows_, cols_):
        return pl.BlockSpec((None, None, rows_, cols_), lambda i: (layer, which, 0, 0), pipeline_mode=pl.Buffered(1))

    in_specs = [
        _tok_spec(D_MODEL, tm),
        _mod_spec(tm, tokens_per_row),
        _resident((1, D_MODEL)),
        stack_spec(D_MODEL, D_FF),
        stack_spec(D_MODEL, D_FF),
        stack_spec(D_FF, D_MODEL),
    ] + [_tok_spec(a.shape[1], tm) for a in acts]
    args = [x2d, mrows, ng, w1, w3, w2, *acts]
    if acts:
        in_specs.append(_resident(w_out.shape))
        args.append(w_out)
    return pl.pallas_call(
        functools.partial(_ffn_kernel, mi=mi, n_act=len(acts)),
        grid=(n // tm,),
        in_specs=in_specs,
        out_specs=_tok_spec(D_MODEL, tm),
        out_shape=jax.ShapeDtypeStruct((n, D_MODEL), F32),
        scratch_shapes=[pltpu.VMEM((n_sub, TOKEN_TILE, D_MODEL), F32), pltpu.VMEM((TOKEN_TILE, D_FF), BF16)],
        compiler_params=_cparams("arbitrary"),
        name="swiglu",
    )(*args)


def _even_in_kernel(*refs, latent):
    if latent:
        x_ref, m_ref, ng_ref, w_ref, qg_ref, kg_ref, cos_ref, sin_ref = refs[:8]
        outs = refs[8:]
    else:
        x_ref, m_ref, ng_ref, w_ref, qg_ref, kg_ref = refs[:6]
        outs = refs[6:]
    xr_ref, gt_ref, q_ref, k_ref, v_ref = outs
    base = 2 * D_RNN

    def rows(s):
        return slice(s * SUB_TILE, (s + 1) * SUB_TILE)

    def prologue(s):
        return _rms_mod(x_ref[rows(s), :], ng_ref[...], m_ref[3:4, :], m_ref[4:5, :]).astype(BF16)

    def matmuls(s, h):
        r = rows(s)
        xr_ref[r, :] = _dot(h, w_ref[:, 0:D_RNN])
        gt_ref[r, :] = _dot(h, w_ref[:, D_RNN:2 * D_RNN])
        v_ref[r, :] = _dot(h, w_ref[:, base + 2 * DIFF_W:base + 3 * DIFF_W]).astype(v_ref.dtype)
        return _dot(h, w_ref[:, base:base + DIFF_W]), _dot(h, w_ref[:, base + DIFF_W:base + 2 * DIFF_W])

    def epilogue(s, qk):
        q, k = qk
        r = rows(s)
        for hh in range(DIFF_HEADS):
            cols = slice(hh * LANES, (hh + 1) * LANES)
            qh = _head_norm(q[:, cols], qg_ref[...])
            kh = _head_norm(k[:, cols], kg_ref[...])
            if latent:
                qh = _rope(qh, cos_ref[r, :], sin_ref[r, :])
                kh = _rope(kh, cos_ref[r, :], sin_ref[r, :])
            q_ref[r, cols] = (qh * Q_SCALE).astype(BF16)
            k_ref[r, cols] = kh.astype(k_ref.dtype)

    _staggered(x_ref.shape[0] // SUB_TILE, prologue, matmuls, epilogue)


def _even_in(x2d, mrows, tokens_per_row, ng, w_in, qg, kg, rope):
    n = x2d.shape[0]
    tm = IN_TILE
    latent = rope is not None
    in_specs = [
        _tok_spec(D_MODEL, tm), _mod_spec(tm, tokens_per_row), _resident((1, D_MODEL)),
        _resident((D_MODEL, EVEN_IN)), _resident((1, LANES)), _resident((1, LANES)),
    ]
    args = [x2d, mrows, ng, w_in, qg, kg]
    if latent:
        seq = rope[0].shape[0]
        tab = pl.BlockSpec((tm, LANES), lambda i: (i % (seq // tm), 0))
        in_specs += [tab, tab]
        args += list(rope)
    kv_dtype = BF16 if latent else F32
    return pl.pallas_call(
        functools.partial(_even_in_kernel, latent=latent),
        grid=(n // tm,),
        in_specs=in_specs,
        out_specs=[_tok_spec(D_RNN, tm)] * 2 + [_tok_spec(DIFF_W, tm)] * 3,
        out_shape=[
            jax.ShapeDtypeStruct((n, D_RNN), F32), jax.ShapeDtypeStruct((n, D_RNN), F32),
            jax.ShapeDtypeStruct((n, DIFF_W), BF16),
            jax.ShapeDtypeStruct((n, DIFF_W), kv_dtype), jax.ShapeDtypeStruct((n, DIFF_W), kv_dtype),
        ],
        compiler_params=_cparams("arbitrary"),
        name="even_in",
    )(*args)


def _gelu_tanh(x):
    return 0.5 * x * (1.0 + jnp.tanh(math.sqrt(2.0 / math.pi) * (x + 0.044715 * (x * x * x))))


def _lru_kernel(xr_ref, gt_ref, cw_ref, cb_ref, wg_ref, bg_ref, lam_ref, h0_ref,
                y_ref, last_ref, hf_scr, hb_scr, xc_scr, af0, uf0, ab0, ub0, af1, uf1, ab1, ub1, *, seq, tt):
    n_chunks = seq // tt
    rows = tt * SUBLANES
    lam = lam_ref[...]
    m2sp = (0.25 * LRU_C) * (jnp.maximum(-lam, 0.0) + jnp.log1p(jnp.exp(-jnp.abs(lam))))

    def time_major(ref, t0, n):
        return jnp.swapaxes(ref[:, pl.ds(pl.multiple_of(t0, SUBLANES), n), :], 0, 1)

    def conv_chunk(c):
        t0 = c * tt
        before = time_major(xr_ref, jnp.maximum(t0 - SUBLANES, 0), SUBLANES)[SUBLANES - 2:]
        after = time_major(xr_ref, jnp.minimum(t0 + tt, seq - SUBLANES), SUBLANES)[:1]
        before = jnp.where(c > 0, before, 0.0)
        after = jnp.where(c < n_chunks - 1, after, 0.0)
        xw = jnp.concatenate([before, time_major(xr_ref, t0, tt), after], axis=0)
        xc = cb_ref[...]
        for tap in range(4):
            xc = xc + xw[tap:tap + tt] * cw_ref[tap:tap + 1, :]
        return xc.reshape(rows, LANES)

    def decay_and_input(th_r, th_i, xc, m2sp_row):
        nt = jnp.tanh(m2sp_row * th_r + m2sp_row)
        inv = 1.0 / (1.0 + nt)
        a = (1.0 - nt) * inv
        root = jnp.where(nt > 0.0, nt * lax.rsqrt(nt), 0.0)
        u = (root * inv) * ((th_i + 1.0) * xc)
        return a.reshape(tt, SUBLANES, LANES), u.reshape(tt, SUBLANES, LANES)

    def conv_pass(c, _):
        xc_scr[pl.ds(pl.multiple_of(c * rows, rows), rows), :] = conv_chunk(c)
        return 0

    lax.fori_loop(0, n_chunks, conv_pass, 0)

    def gates(c, d):
        xc = xc_scr[pl.ds(pl.multiple_of(c * rows, rows), rows), :]
        cols = slice(2 * d * LANES, 2 * (d + 1) * LANES)
        th = jnp.tanh(_dot(xc.astype(BF16), wg_ref[:, cols]) + bg_ref[:, cols])
        return decay_and_input(th[:, :LANES], th[:, LANES:], xc, m2sp[d:d + 1, :])

    bufs = ((af0, uf0, ab0, ub0), (af1, uf1, ab1, ub1))

    def fill(c, buf):
        buf[0][...], buf[1][...] = gates(c, 0)
        buf[2][...], buf[3][...] = gates(n_chunks - 1 - c, 1)

    def scan(c, buf, carry):
        af, uf, ab, ub = buf
        hf, hb = carry
        f0 = c * tt
        b0 = (n_chunks - 1 - c) * tt
        for i in range(tt):
            hf = af[i] * hf + uf[i]
            hf_scr[f0 + i] = hf
            ib = tt - 1 - i
            hb = ab[ib] * hb + ub[ib]
            hb_scr[b0 + ib] = hb
        return hf, hb

    def two_trips(k, carry):
        c = 2 * k
        fill(c + 1, bufs[1])
        carry = scan(c, bufs[0], carry)
        fill(jnp.minimum(c + 2, n_chunks - 1), bufs[0])
        return scan(c + 1, bufs[1], carry)

    fill(0, bufs[0])
    hf, hb = lax.fori_loop(0, n_chunks // 2, two_trips, (h0_ref[:, 0, :], h0_ref[:, 1, :]))
    last_ref[:, 0, :] = hf
    last_ref[:, 1, :] = hb

    def finish(c, _):
        t0 = pl.multiple_of(c * tt, tt)
        h = jnp.swapaxes(hf_scr[pl.ds(t0, tt)] + hb_scr[pl.ds(t0, tt)], 0, 1)
        y_ref[:, pl.ds(t0, tt), :] = (h * _gelu_tanh(gt_ref[:, pl.ds(t0, tt), :])).astype(y_ref.dtype)
        return 0

    lax.fori_loop(0, n_chunks, finish, 0)


def _lru(xr, gt, conv_w, conv_b, wg, bg, lam, h0):
    b, seq, _ = xr.shape
    tt = min(LRU_STEPS, seq)
    blk_bytes = seq * SUBLANES * LANES * 4
    mode = dict(pipeline_mode=pl.Buffered(1)) if blk_bytes > LRU_DOUBLE_BUFFER_MAX else {}
    seq_in = pl.BlockSpec((SUBLANES, seq, LANES), lambda i, j: (i, 0, j), **mode)
    seq_out = pl.BlockSpec((SUBLANES, seq, LANES), lambda i, j: (i, 0, j))
    st_spec = pl.BlockSpec((SUBLANES, 2, LANES), lambda i, j: (i, 0, j))
    return pl.pallas_call(
        functools.partial(_lru_kernel, seq=seq, tt=tt),
        grid=(b // SUBLANES, D_RNN // LANES),
        in_specs=[seq_in, seq_in,
                  pl.BlockSpec((4, LANES), lambda i, j: (0, j)), pl.BlockSpec((1, LANES), lambda i, j: (0, j)),
                  pl.BlockSpec((None, LANES, 4 * LANES), lambda i, j: (j, 0, 0)),
                  pl.BlockSpec((None, 1, 4 * LANES), lambda i, j: (j, 0, 0)),
                  pl.BlockSpec((2, LANES), lambda i, j: (0, j)), st_spec],
        out_specs=[seq_out, st_spec],
        out_shape=[jax.ShapeDtypeStruct((b, seq, D_RNN), BF16), jax.ShapeDtypeStruct((b, 2, D_RNN), F32)],
        scratch_shapes=[pltpu.VMEM((seq, SUBLANES, LANES), F32)] * 2 + [pltpu.VMEM((seq * SUBLANES, LANES), F32)]
        + [pltpu.VMEM((tt, SUBLANES, LANES), F32)] * 8,
        compiler_params=_cparams("arbitrary", "arbitrary"),
        name="rglru",
    )(xr, gt, conv_w, conv_b, wg, bg, lam, h0)


def _diff_attn_kernel(*refs, n_src, lam_init):
    q_ref, lamv_ref, sg_ref = refs[:3]
    o_ref = refs[3 + 2 * n_src]
    scr = refs[4 + 2 * n_src:]
    for bi in range(q_ref.shape[0]):
        srcs = [(refs[3 + 2 * s].at[bi], refs[4 + 2 * s].at[bi]) for s in range(n_src)]
        _diff_attn_one(q_ref.at[bi], lamv_ref, sg_ref, srcs, o_ref.at[bi], scr, n_src, lam_init)


def _diff_attn_one(q_ref, lamv_ref, sg_ref, srcs, o_ref, scr, n_src, lam_init):
    kb = [scr[2 * s] for s in range(n_src)]
    vt = [scr[2 * s + 1] for s in range(n_src)]

    @pl.when(pl.program_id(1) == 0)
    def _():
        for s, (k_ref, v_ref) in enumerate(srcs):
            kb[s][...] = k_ref[...].astype(BF16)
            for hh in range(DIFF_HEADS):
                cols = slice(hh * LANES, (hh + 1) * LANES)
                vt[s][hh, 0:LANES, :] = v_ref[:, cols].astype(F32).T.astype(BF16)
                vt[s][hh, LANES:, :] = jnp.ones((ONES_ROWS, v_ref.shape[0]), BF16)

    lv = lamv_ref[...]
    lam = (jnp.exp(jnp.sum(lv[0:1, :] * lv[1:2, :], axis=-1, keepdims=True))
           - jnp.exp(jnp.sum(lv[2:3, :] * lv[3:4, :], axis=-1, keepdims=True)) + lam_init)
    lo = _lo_mask()
    zero = jnp.zeros((), BF16)
    tq = q_ref.shape[0]

    chunks = []
    for s, (k_ref, _) in enumerate(srcs):
        t_s = k_ref.shape[0]
        step = min(DIFF_KEY_CHUNK, t_s)
        chunks += [(s, t0, step) for t0 in range(0, t_s, step)]
    stages = [(hh, ci) for hh in range(DIFF_HEADS) for ci in range(len(chunks))]

    def masked_queries(hh):
        qh = q_ref[:, hh * LANES:(hh + 1) * LANES]
        return jnp.concatenate([jnp.where(lo, qh, zero), jnp.where(lo, zero, qh)], axis=0)

    qs = [masked_queries(hh) for hh in range(DIFF_HEADS)]

    def scores(hh, ci):
        s, t0, n = chunks[ci]
        return _dot_nt(kb[s][t0:t0 + n, hh * LANES:(hh + 1) * LANES], qs[hh])

    queue = [scores(*stages[i]) for i in range(min(DIFF_LOOKAHEAD, len(stages)))]
    m = acc = None
    for idx, (hh, ci) in enumerate(stages):
        st = queue.pop(0)
        if idx + DIFF_LOOKAHEAD < len(stages):
            queue.append(scores(*stages[idx + DIFF_LOOKAHEAD]))
        s, t0, n = chunks[ci]
        vt_c = vt[s][hh, :, t0:t0 + n]
        m_c = jnp.max(st, axis=0, keepdims=True)
        if ci == 0:
            m = m_c
            acc = _dot(vt_c, jnp.exp2(st - m).astype(BF16))
        else:
            m_new = jnp.maximum(m, m_c)
            acc = acc * jnp.exp2(m - m_new) + _dot(vt_c, jnp.exp2(st - m_new).astype(BF16))
            m = m_new
        if ci == len(chunks) - 1:
            cols = slice(hh * LANES, (hh + 1) * LANES)
            o2t = acc[0:LANES, :] * (1.0 / acc[LANES:LANES + 1, :])
            o = (o2t[:, :tq] - lam * o2t[:, tq:]).T
            ms = jnp.mean(o * o, axis=-1, keepdims=True)
            o = o * lax.rsqrt(ms + EPS) * sg_ref[...] * (1.0 - lam_init)
            o_ref[:, cols] = o.astype(o_ref.dtype)


def _diff_attn(q, lam_vec, subln_g, srcs, lam_init):
    b, seq, _ = q.shape
    tq = min(DIFF_Q_TILE, seq)
    bt = _batch_group(b, seq)
    q_spec = pl.BlockSpec((bt, tq, DIFF_W), lambda i, j: (i, j, 0))
    in_specs = [q_spec, _resident((4, DIFF_HD)), _resident((1, LANES))]
    args = [q, lam_vec, subln_g]
    scratch = []
    for k, v in srcs:
        kv_spec = pl.BlockSpec((bt, k.shape[1], DIFF_W), lambda i, j: (i, 0, 0))
        in_specs += [kv_spec, kv_spec]
        args += [k, v]
        scratch += [pltpu.VMEM((k.shape[1], DIFF_W), BF16),
                    pltpu.VMEM((DIFF_HEADS, LANES + ONES_ROWS, k.shape[1]), BF16)]
    return pl.pallas_call(
        functools.partial(_diff_attn_kernel, n_src=len(srcs), lam_init=lam_init),
        grid=(b // bt, seq // tq),
        in_specs=in_specs,
        out_specs=q_spec,
        out_shape=jax.ShapeDtypeStruct((b, seq, DIFF_W), BF16),
        scratch_shapes=scratch,
        compiler_params=_cparams("arbitrary", "arbitrary"),
        name="diff_attn",
    )(*args)


def _odd_in_kernel(*refs, latent):
    if latent:
        x_ref, m_ref, ng_ref, w_ref, qg_ref, kg_ref, ones_ref, cos_ref, sin_ref = refs[:9]
        q_ref, kd_ref, v_ref = refs[9:]
    else:
        x_ref, m_ref, ng_ref, w_ref, qg_ref, kg_ref, ones_ref = refs[:7]
        q_ref, kd_ref, k_ref, v_ref = refs[7:]
    n_q = WIN_HEADS * WIN_HD
    n_kv = WIN_KV * WIN_HD

    def rows(s):
        return slice(s * SUB_TILE, (s + 1) * SUB_TILE)

    def prologue(s):
        return _rms_mod(x_ref[rows(s), :], ng_ref[...], m_ref[3:4, :], m_ref[4:5, :]).astype(BF16)

    def matmuls(s, h):
        v_ref[rows(s), :] = _dot(h, w_ref[:, n_q + n_kv:n_q + 2 * n_kv]).astype(v_ref.dtype)
        return _dot(h, w_ref[:, 0:n_q]), _dot(h, w_ref[:, n_q:n_q + n_kv])

    def epilogue(s, qk):
        q, k = qk
        r = rows(s)
        for pair in range(n_q // (2 * LANES)):
            qn = _head_norm_mxu(q[:, 2 * pair * LANES:2 * (pair + 1) * LANES], qg_ref[...], ones_ref[...])
            for half in range(2):
                cols = slice((2 * pair + half) * LANES, (2 * pair + half + 1) * LANES)
                qh = qn[:, half * LANES:(half + 1) * LANES]
                if latent:
                    qh = _rope(qh, cos_ref[r, :], sin_ref[r, :])
                q_ref[r, cols] = (qh * Q_SCALE).astype(BF16)
        kn = _head_norm_mxu(k, kg_ref[...], ones_ref[...])
        for blk in range(n_kv // LANES):
            cols = slice(blk * LANES, (blk + 1) * LANES)
            kh = kn[:, cols]
            if latent:
                kh = _rope(kh, cos_ref[r, :], sin_ref[r, :])
            else:
                k_ref[r, cols] = kh
            for half in range(2):
                dst = slice((2 * blk + half) * LANES, (2 * blk + half + 1) * LANES)
                kd_ref[r, dst] = _dup_half(kh, half).astype(BF16)

    _staggered(x_ref.shape[0] // SUB_TILE, prologue, matmuls, epilogue)


def _odd_in(x2d, mrows, tokens_per_row, ng, w_in, qg, kg, rope):
    n = x2d.shape[0]
    tm = IN_TILE
    latent = rope is not None
    n_q = WIN_HEADS * WIN_HD
    n_kv = WIN_KV * WIN_HD
    in_specs = [
        _tok_spec(D_MODEL, tm), _mod_spec(tm, tokens_per_row), _resident((1, D_MODEL)),
        _resident((D_MODEL, ODD_IN)), _resident((1, 2 * LANES)), _resident((1, 2 * LANES)),
        _resident((2 * LANES, 2 * LANES)),
    ]
    args = [x2d, mrows, ng, w_in, qg, kg, _head_ones()]
    out_specs = [_tok_spec(n_q, tm), _tok_spec(2 * n_kv, tm)]
    out_shape = [jax.ShapeDtypeStruct((n, n_q), BF16), jax.ShapeDtypeStruct((n, 2 * n_kv), BF16)]
    if latent:
        seq = rope[0].shape[0]
        tab = pl.BlockSpec((tm, LANES), lambda i: (i % (seq // tm), 0))
        in_specs += [tab, tab]
        args += list(rope)
        out_specs += [_tok_spec(n_kv, tm)]
        out_shape += [jax.ShapeDtypeStruct((n, n_kv), BF16)]
    else:
        out_specs += [_tok_spec(n_kv, tm)] * 2
        out_shape += [jax.ShapeDtypeStruct((n, n_kv), F32)] * 2
    return pl.pallas_call(
        functools.partial(_odd_in_kernel, latent=latent),
        grid=(n // tm,),
        in_specs=in_specs,
        out_specs=out_specs,
        out_shape=out_shape,
        compiler_params=_cparams("arbitrary"),
        name="odd_in",
    )(*args)


def _win_attn_kernel(*refs, latent, seq, tq):
    n_batched = 6 if latent else 4
    for bi in range(refs[0].shape[0]):
        one = [r if idx == 1 else r.at[bi] for idx, r in enumerate(refs[:n_batched + 1])]
        _win_attn_one(*one, *refs[n_batched + 1:], latent=latent, seq=seq, tq=tq)


def _win_attn_one(*refs, latent, seq, tq):
    if latent:
        q_ref, sink_ref, kd_ref, v_ref, ck_ref, cv_ref, o_ref, vt_scr, ckd_scr, cvt_scr = refs
    else:
        q_ref, sink_ref, kd_ref, v_ref, o_ref, vt_scr = refs
    lo = _lo_mask()
    zero = jnp.zeros((), BF16)
    n_cols = WIN_G * tq
    n_kv = WIN_KV * WIN_HD

    @pl.when(pl.program_id(1) == 0)
    def _():
        srcs = [(v_ref, vt_scr)] + ([(cv_ref, cvt_scr)] if latent else [])
        for src_ref, dst_scr in srcs:
            for blk in range(n_kv // LANES):
                vt2 = src_ref[:, blk * LANES:(blk + 1) * LANES].astype(F32).T.astype(BF16)
                for half in range(2):
                    dst_scr[2 * blk + half, 0:WIN_HD, :] = vt2[half * WIN_HD:(half + 1) * WIN_HD, :]
            for j in range(WIN_KV):
                dst_scr[j, WIN_HD:, :] = jnp.ones((ONES_ROWS, src_ref.shape[0]), BF16)
        if latent:
            for j in range(WIN_KV):
                cblk = slice((j // 2) * LANES, (j // 2 + 1) * LANES)
                ckd_scr[:, j * LANES:(j + 1) * LANES] = _dup_half(ck_ref[:, cblk], j % 2).astype(BF16)

    n_tiles = q_ref.shape[0] // tq
    starts, biases = [], []
    if latent:
        span = 3 * tq
        for t in range(n_tiles):
            i = pl.program_id(1) * n_tiles + t
            start = pl.multiple_of(jnp.clip((i - 1) * tq, 0, seq - span), tq)
            kpos = start + lax.broadcasted_iota(jnp.int32, (span, 1), 0)
            qpos = i * tq + (lax.broadcasted_iota(jnp.int32, (1, n_cols), 1) & (tq - 1))
            starts.append(start)
            biases.append(jnp.where(jnp.abs(kpos - qpos) <= WINDOW, 0.0, NEG_INF))

    def scores(t, j):
        kcols = slice(j * LANES, (j + 1) * LANES)
        qs = []
        for pair in range(WIN_G // 2):
            qb = q_ref[t * tq:(t + 1) * tq, (2 * j + pair) * LANES:(2 * j + pair + 1) * LANES]
            qs += [jnp.where(lo, qb, zero), jnp.where(lo, zero, qb)]
        qs = jnp.concatenate(qs, axis=0)
        if latent:
            return [_dot_nt(kd_ref[pl.ds(starts[t], span), kcols], qs) + biases[t], _dot_nt(ckd_scr[:, kcols], qs)]
        return [_dot_nt(kd_ref[:, kcols], qs)]

    stages = [(t, j) for t in range(n_tiles) for j in range(WIN_KV)]
    queue = [scores(*stages[n]) for n in range(min(WIN_LOOKAHEAD, len(stages)))]
    for n, (t, j) in enumerate(stages):
        sts = queue.pop(0)
        if n + WIN_LOOKAHEAD < len(stages):
            queue.append(scores(*stages[n + WIN_LOOKAHEAD]))
        trows = slice(t * tq, (t + 1) * tq)
        sink = jnp.concatenate([jnp.full((1, tq), sink_ref[WIN_G * j + g] * LOG2E, F32) for g in range(WIN_G)],
                               axis=1)
        vts = [vt_scr[j, :, pl.ds(starts[t], span)], cvt_scr[j]] if latent else [vt_scr[j]]
        m = functools.reduce(jnp.maximum, [jnp.max(st, axis=0, keepdims=True) for st in sts])
        m = jnp.maximum(m, sink)
        ps = [jnp.exp2(st - m) for st in sts]
        ot = functools.reduce(jnp.add, [_dot(vt, p.astype(BF16)) for vt, p in zip(vts, ps)])
        den = ot[WIN_HD:WIN_HD + 1, :] + jnp.exp2(sink - m)
        ot = ot[0:WIN_HD, :] * (1.0 / den)
        for pair in range(WIN_G // 2):
            blk = 2 * j + pair
            c0 = 2 * pair * tq
            both = jnp.concatenate([ot[:, c0:c0 + tq], ot[:, c0 + tq:c0 + 2 * tq]], axis=0)
            o_ref[trows, blk * LANES:(blk + 1) * LANES] = both.T.astype(o_ref.dtype)


def _win_attn(q, sink, kd, v, ctx):
    b, seq, n_q = q.shape
    n_kv = v.shape[2]
    latent = ctx is not None
    tq = WIN_Q_TILE if latent else min(seq, 2 * WIN_Q_TILE)
    step_rows = WIN_TILES_PER_STEP * tq if latent else tq
    bt = _batch_group(b, seq)
    q_spec = pl.BlockSpec((bt, step_rows, n_q), lambda i, j: (i, j, 0))
    kd_spec = pl.BlockSpec((bt, seq, kd.shape[2]), lambda i, j: (i, 0, 0))
    v_spec = pl.BlockSpec((bt, seq, n_kv), lambda i, j: (i, 0, 0))
    in_specs = [q_spec, pl.BlockSpec(memory_space=pltpu.SMEM), kd_spec, v_spec]
    args = [q, sink, kd, v]
    scratch = [pltpu.VMEM((WIN_KV, WIN_HD + ONES_ROWS, seq), BF16)]
    if latent:
        ck, cv = ctx
        past = ck.shape[1]
        c_spec = pl.BlockSpec((bt, past, n_kv), lambda i, j: (i, 0, 0))
        in_specs += [c_spec, c_spec]
        args += [ck, cv]
        scratch += [pltpu.VMEM((past, kd.shape[2]), BF16), pltpu.VMEM((WIN_KV, WIN_HD + ONES_ROWS, past), BF16)]
    return pl.pallas_call(
        functools.partial(_win_attn_kernel, latent=latent, seq=seq, tq=tq),
        grid=(b // bt, seq // step_rows),
        in_specs=in_specs,
        out_specs=q_spec,
        out_shape=jax.ShapeDtypeStruct((b, seq, n_q), BF16),
        scratch_shapes=scratch,
        compiler_params=_cparams("arbitrary", "arbitrary"),
        name="win_attn",
    )(*args)


def _rope_tables(seq):
    rows = seq // GRID_W
    row = np.repeat(np.arange(rows), GRID_W).astype(np.float32)
    col = np.tile(np.arange(GRID_W), rows).astype(np.float32)
    half = DIFF_HD // 2
    inv = (ROPE_THETA ** (-np.arange(0, half, 2, dtype=np.float32) / half)).astype(np.float32)
    ang_r = row[:, None] * inv[None, :]
    ang_c = col[:, None] * inv[None, :]
    cos = np.concatenate([np.cos(ang_r), np.cos(ang_r), np.cos(ang_c), np.cos(ang_c)], axis=1)
    sin = np.concatenate([-np.sin(ang_r), np.sin(ang_r), -np.sin(ang_c), np.sin(ang_c)], axis=1)
    cos = np.tile(cos, (1, LANES // DIFF_HD)).astype(np.float32)
    sin = np.tile(sin, (1, LANES // DIFF_HD)).astype(np.float32)
    return jnp.asarray(cos), jnp.asarray(sin)


def _block_diag(w):
    eye = jnp.eye(RNN_BLOCKS, dtype=w.dtype)
    return jnp.einsum('nkj,nm->nkmj', w, eye).reshape(D_RNN, D_RNN)


def _gate_params(wa, ba, wi, bi):
    mats = [_block_diag(wa[0]), _block_diag(wi[0]), _block_diag(wa[1]), _block_diag(wi[1])]
    vecs = [ba[0], bi[0], ba[1], bi[1]]
    w_blocks, b_blocks = [], []
    for blk in range(D_RNN // LANES):
        sl = slice(blk * LANES, (blk + 1) * LANES)
        w_blocks.append(jnp.concatenate([m[sl, sl] for m in mats], axis=1))
        b_blocks.append(jnp.concatenate([v[sl] for v in vecs]).reshape(1, -1))
    return (0.5 * jnp.stack(w_blocks)).astype(BF16), 0.5 * jnp.stack(b_blocks)


def _tile_gain(g, width=LANES):
    return jnp.tile(g, width // g.shape[0]).reshape(1, width)


def _head_ones():
    head = np.arange(2 * LANES) // WIN_HD
    return jnp.asarray((head[:, None] == head[None, :]).astype(np.float32), dtype=BF16)


def _diff_lambda_init(layer):
    return 0.8 - 0.6 * math.exp(-0.3 * layer)


def kernel(x_prompt, x_sample, cache_diff_k, cache_diff_v, state_lru, cache_win_k, cache_win_v, c, c_ctx,
           norm_g, w_mod, b_mod, ffn_w1, ffn_w3, ffn_w2, e_w_in, e_w_out, e_conv_w, e_conv_b,
           e_lru_wa, e_lru_ba, e_lru_wi, e_lru_bi, e_lru_lam, e_q_g, e_k_g, e_lam, e_subln_g,
           o_w_in, o_w_out, o_q_g, o_k_g, o_sink):
    batch, seq, _ = x_prompt.shape
    dec_batch, dec_seq, _ = x_sample.shape
    past = cache_diff_k.shape[2]

    cond = jnp.concatenate([c_ctx[None, :], c], axis=0)
    cond = jnp.pad(cond, ((0, COND_ROWS - cond.shape[0]), (0, 0)))
    mod = _modulation(cond, w_mod, b_mod).reshape(DEPTH, COND_ROWS, N_MOD, D_MODEL)

    w1 = ffn_w1.astype(BF16)
    w3 = ffn_w3.astype(BF16)
    w2 = ffn_w2.astype(BF16)
    rope = _rope_tables(dec_seq)

    groups = [
        dict(x=x_prompt.reshape(batch * seq, D_MODEL), b=batch, s=seq, latent=False,
             rows=slice(0, 1), per_row=batch * seq),
        dict(x=x_sample.reshape(dec_batch * dec_seq, D_MODEL), b=dec_batch, s=dec_seq, latent=True,
             rows=slice(1, 1 + dec_batch), per_row=dec_seq),
    ]
    ctx_out = {}
    finals = []
    for grp in groups:
        x = grp['x']
        nb, s, latent, per_row = grp['b'], grp['s'], grp['latent'], grp['per_row']
        for l in range(DEPTH):
            j = l // 2
            mrows = mod[l, grp['rows']]
            ng = norm_g[l].reshape(3, 1, D_MODEL)
            x = _ffn(x, mrows, per_row, ng[0], w1, w3, w2, l, 0)
            if l % 2 == 0:
                xr, gt, q, k, v = _even_in(x, mrows, per_row, ng[1], e_w_in[j].astype(BF16),
                                           _tile_gain(e_q_g[j]), _tile_gain(e_k_g[j]), rope if latent else None)
                wg, bg = _gate_params(e_lru_wa[j], e_lru_ba[j], e_lru_wi[j], e_lru_bi[j])
                h0 = state_lru[:, j] if latent else jnp.zeros((nb, 2, D_RNN), F32)
                y_rnn, last = _lru(xr.reshape(nb, s, D_RNN), gt.reshape(nb, s, D_RNN), e_conv_w[j],
                                   e_conv_b[j].reshape(1, D_RNN), wg, bg, e_lru_lam[j], h0)
                k3 = k.reshape(nb, s, DIFF_W)
                v3 = v.reshape(nb, s, DIFF_W)
                srcs = [(k3, v3)]
                if latent:
                    srcs = [(cache_diff_k[:, j].reshape(nb, past, DIFF_W),
                             cache_diff_v[:, j].reshape(nb, past, DIFF_W))] + srcs
                else:
                    ctx_out.setdefault('diff_k', []).append(k3.reshape(nb, s, DIFF_HEADS, 2 * DIFF_HD))
                    ctx_out.setdefault('diff_v', []).append(v3.reshape(nb, s, DIFF_HEADS, 2 * DIFF_HD))
                    ctx_out.setdefault('state', []).append(last)
                o = _diff_attn(q.reshape(nb, s, DIFF_W), e_lam[j], e_subln_g[j].reshape(1, LANES), srcs,
                               _diff_lambda_init(l))
                acts = [y_rnn.reshape(nb * s, D_RNN), o.reshape(nb * s, DIFF_W)]
                w_out = e_w_out[j].astype(BF16)
            else:
                outs = _odd_in(x, mrows, per_row, ng[1], o_w_in[j].astype(BF16),
                               _tile_gain(o_q_g[j], 2 * LANES), _tile_gain(o_k_g[j], 2 * LANES),
                               rope if latent else None)
                q, kd = outs[:2]
                v = outs[-1]
                n_kv = WIN_KV * WIN_HD
                ctx = None
                if latent:
                    ctx = (cache_win_k[:, j].reshape(nb, past, n_kv), cache_win_v[:, j].reshape(nb, past, n_kv))
                else:
                    ctx_out.setdefault('win_k', []).append(outs[2].reshape(nb, s, WIN_KV, WIN_HD))
                    ctx_out.setdefault('win_v', []).append(v.reshape(nb, s, WIN_KV, WIN_HD))
                o = _win_attn(q.reshape(nb, s, ODD_MIX), o_sink[j], kd.reshape(nb, s, 2 * n_kv),
                              v.reshape(nb, s, n_kv), ctx)
                acts = [o.reshape(nb * s, ODD_MIX)]
                w_out = o_w_out[j].astype(BF16)
            x = _ffn(x, mrows, per_row, ng[2], w1, w3, w2, l, 1, acts, w_out)
        finals.append(x.reshape(nb, s, D_MODEL))

    return (finals[0], finals[1],
            jnp.stack(ctx_out['diff_k'], axis=1), jnp.stack(ctx_out['diff_v'], axis=1),
            jnp.stack(ctx_out['state'], axis=1),
            jnp.stack(ctx_out['win_k'], axis=1), jnp.stack(ctx_out['win_v'], axis=1))
```
